```python
import jax
import jax.numpy as jnp
from jax import lax
import numpy as np


D_MODEL = 1024
BATCH = 2
SEQ = 8192
DEPTH = 1

M_HEADS = 4
M_HEAD_DIM = 128
M_WIDTH = M_HEADS * M_HEAD_DIM
M_CONV = 4
M_CHUNK = 64

A_HEADS = 8
A_HEAD_DIM = 64
A_WIDTH = A_HEADS * A_HEAD_DIM
MOBA_BLOCK = 256
MOBA_TOPK = 3
Q_BLOCK = 64
ROPE_THETA = 10000.0

D_FF = ((8 * D_MODEL + 3 * 256 - 1) // (3 * 256)) * 256

N_MOD = 6
IN_SPLITS = (M_WIDTH, M_WIDTH, M_WIDTH, M_WIDTH, M_HEADS, M_HEADS, A_WIDTH, A_WIDTH, A_WIDTH, D_MODEL, D_MODEL)
D_IN = 4 * M_WIDTH + 2 * M_HEADS + 3 * A_WIDTH + 2 * D_MODEL
F_GATE_OFFSET = 4 * M_WIDTH + M_HEADS
RMS_EPS = 1e-6

kernel_name = 'hybrid_mlstm_moba_gated_block'


def rms_norm(x, g):
    xf = x.astype(jnp.float32)
    y = xf * lax.rsqrt(jnp.mean(xf * xf, axis=-1, keepdims=True) + RMS_EPS)
    return (y * g.astype(jnp.float32)).astype(x.dtype)


def modulate(h, shift, scale):
    return h * (1 + scale[:, None, :]) + shift[:, None, :]


def causal_dwconv(x, w, b):
    K = w.shape[0]
    S = x.shape[1]
    xp = jnp.pad(x, ((0, 0), (K - 1, 0), (0, 0)))
    y = b
    for j in range(K):
        y = y + xp[:, j:j + S, :] * w[j]
    return y


def split_heads(t, n_heads, head_dim):
    B, S, _ = t.shape
    return t.reshape(B, S, n_heads, head_dim).transpose(0, 2, 1, 3)


def merge_heads(t):
    B, H, S, Dh = t.shape
    return t.transpose(0, 2, 1, 3).reshape(B, S, H * Dh)


def rope(x, positions):
    half = x.shape[-1] // 2
    inv_freq = ROPE_THETA ** (-jnp.arange(half, dtype=jnp.float32) / half)
    ang = positions.astype(jnp.float32)[:, None, :, None] * inv_freq
    cos, sin = jnp.cos(ang), jnp.sin(ang)
    xf = x.astype(jnp.float32)
    x1, x2 = xf[..., :half], xf[..., half:]
    out = jnp.concatenate([x1 * cos - x2 * sin, x2 * cos + x1 * sin], axis=-1)
    return out.astype(x.dtype)


def mlstm_chunkwise(q, k, v, i_pre, f_pre):
    B, H, S, Dh = q.shape
    L = M_CHUNK
    NC = S // L
    f32 = jnp.float32
    q = q.astype(f32).reshape(B, H, NC, L, Dh)
    k = (k.astype(f32) * (Dh ** -0.5)).reshape(B, H, NC, L, Dh)
    v = v.astype(f32).reshape(B, H, NC, L, Dh)
    log_i = i_pre.astype(f32).reshape(B, H, NC, L)
    log_f = jax.nn.log_sigmoid(f_pre.astype(f32)).reshape(B, H, NC, L)
    b = jnp.cumsum(log_f, axis=-1)
    b_last = b[..., -1]
    a = b_last[..., None] - b + log_i

    def step(carry, inp):
        C, n, m = carry
        a_c, bl_c, k_c, v_c = inp
        m_new = jnp.maximum(bl_c + m, jnp.max(a_c, axis=-1))
        decay = jnp.exp(bl_c + m - m_new)
        w = jnp.exp(a_c - m_new[..., None])
        C_new = decay[..., None, None] * C + jnp.einsum('bhl,bhld,bhle->bhde', w, v_c, k_c)
        n_new = decay[..., None] * n + jnp.einsum('bhl,bhle->bhe', w, k_c)
        return (C_new, n_new, m_new), (C, n, m)

    init = (jnp.zeros((B, H, Dh, Dh), f32), jnp.zeros((B, H, Dh), f32), jnp.zeros((B, H), f32))
    xs = (jnp.moveaxis(a, 2, 0), jnp.moveaxis(b_last, 2, 0), jnp.moveaxis(k, 2, 0), jnp.moveaxis(v, 2, 0))
    _, (C_prev, n_prev, m_prev) = lax.scan(step, init, xs)
    C_prev = jnp.moveaxis(C_prev, 0, 2)
    n_prev = jnp.moveaxis(n_prev, 0, 2)
    m_prev = jnp.moveaxis(m_prev, 0, 2)

    causal = jnp.tril(jnp.ones((L, L), dtype=bool))
    D = b[..., :, None] - b[..., None, :] + log_i[..., None, :]
    D = jnp.where(causal, D, -jnp.inf)
    inter = b + m_prev[..., None]
    m_t = jnp.maximum(inter, jnp.max(D, axis=-1))
    W = jnp.exp(D - m_t[..., None])
    SW = jnp.einsum('bhctd,bhcsd->bhcts', q, k) * W
    inter_w = jnp.exp(inter - m_t)
    num = inter_w[..., None] * jnp.einsum('bhcde,bhcte->bhctd', C_prev, q) + jnp.einsum('bhcts,bhcsd->bhctd', SW, v)
    den = inter_w * jnp.einsum('bhce,bhcte->bhct', n_prev, q) + jnp.sum(SW, axis=-1)
    h = num / jnp.maximum(jnp.abs(den), jnp.exp(-m_t))[..., None]
    return h.reshape(B, H, S, Dh)


def moba_attention(q, k, v):
    B, H, S, Dh = q.shape
    NB = -(-S // MOBA_BLOCK)
    K_SEL = min(MOBA_TOPK, NB)
    pad = NB * MOBA_BLOCK - S
    NQ = S // Q_BLOCK
    scale = Dh ** -0.5
    kp = jnp.pad(k, ((0, 0), (0, 0), (0, pad), (0, 0)))
    vp = jnp.pad(v, ((0, 0), (0, 0), (0, pad), (0, 0)))
    kb = kp.reshape(B, H, NB, MOBA_BLOCK, Dh)
    vb = vp.reshape(B, H, NB, MOBA_BLOCK, Dh)
    kmean = jnp.mean(kb.astype(jnp.float32), axis=3).astype(k.dtype)
    bi = jnp.arange(B)[:, None, None, None]
    hi = jnp.arange(H)[None, :, None, None]
    blk_ids = jnp.arange(NB)

    def one_block(qi):
        q0 = qi * Q_BLOCK
        qc = lax.dynamic_slice_in_dim(q, q0, Q_BLOCK, axis=2)
        cur = q0 // MOBA_BLOCK
        gate = jnp.einsum('bhqd,bhnd->bhqn', qc, kmean).astype(jnp.float32)
        gate = jnp.where(blk_ids < cur, gate, -jnp.inf)
        _, gidx = lax.top_k(gate, K_SEL)
        valid = gidx < cur
        kg = kb[bi, hi, gidx]
        vg = vb[bi, hi, gidx]
        s_sel = jnp.einsum('bhqd,bhqjkd->bhqjk', qc, kg).astype(jnp.float32) * scale
        s_sel = jnp.where(valid[..., None], s_sel, -jnp.inf).reshape(B, H, Q_BLOCK, K_SEL * MOBA_BLOCK)
        k_own = lax.dynamic_slice_in_dim(kp, cur * MOBA_BLOCK, MOBA_BLOCK, axis=2)
        v_own = lax.dynamic_slice_in_dim(vp, cur * MOBA_BLOCK, MOBA_BLOCK, axis=2)
        s_own = jnp.einsum('bhqd,bhkd->bhqk', qc, k_own).astype(jnp.float32) * scale
        qpos = q0 + jnp.arange(Q_BLOCK)
        kpos = cur * MOBA_BLOCK + jnp.arange(MOBA_BLOCK)
        s_own = jnp.where(kpos[None, :] <= qpos[:, None], s_own, -jnp.inf)
        p = jax.nn.softmax(jnp.concatenate([s_sel, s_own], axis=-1), axis=-1).astype(v.dtype)
        p_sel = p[..., :K_SEL * MOBA_BLOCK].reshape(B, H, Q_BLOCK, K_SEL, MOBA_BLOCK)
        p_own = p[..., K_SEL * MOBA_BLOCK:]
        return jnp.einsum('bhqjk,bhqjkd->bhqd', p_sel, vg) + jnp.einsum('bhqk,bhkd->bhqd', p_own, v_own)

    outs = lax.map(one_block, jnp.arange(NQ))
    return outs.transpose(1, 2, 0, 3, 4).reshape(B, H, S, Dh)


def setup_inputs(seed: int = 0) -> dict:
    key = jax.random.key(seed)
    ks = jax.random.split(key, 20)
    f32 = jnp.float32

    def nrm(k, shape, scale):
        return jax.random.normal(k, shape, f32) * scale

    x = nrm(ks[0], (BATCH, SEQ, D_MODEL), 1.0)
    c = nrm(ks[1], (BATCH, D_MODEL), 1.0)
    positions = jnp.broadcast_to(jnp.arange(SEQ, dtype=jnp.int32), (BATCH, SEQ))
    ada_w = nrm(ks[2], (DEPTH, D_MODEL, N_MOD * D_MODEL), 0.5 * D_MODEL ** -0.5)
    ada_b = nrm(ks[3], (DEPTH, N_MOD * D_MODEL), 0.02)
    norm1_g = 1.0 + nrm(ks[4], (DEPTH, D_MODEL), 0.02)
    norm2_g = 1.0 + nrm(ks[5], (DEPTH, D_MODEL), 0.02)
    normf_g = 1.0 + nrm(ks[6], (D_MODEL,), 0.02)
    w_in = nrm(ks[7], (DEPTH, D_MODEL, D_IN), D_MODEL ** -0.5)
    b_in = nrm(ks[8], (DEPTH, D_IN), 0.02)
    b_in = b_in.at[:, F_GATE_OFFSET:F_GATE_OFFSET + M_HEADS].add(jnp.linspace(3.0, 6.0, M_HEADS, dtype=f32))
    conv_w = nrm(ks[9], (DEPTH, M_CONV, 2 * M_WIDTH), M_CONV ** -0.5)
    conv_b = nrm(ks[10], (DEPTH, 2 * M_WIDTH), 0.02)
    m_norm_g = 1.0 + nrm(ks[11], (DEPTH, M_WIDTH), 0.02)
    p_mlstm = nrm(ks[12], (DEPTH, M_WIDTH, D_MODEL), M_WIDTH ** -0.5)
    p_moba = nrm(ks[13], (DEPTH, A_WIDTH, D_MODEL), A_WIDTH ** -0.5)
    w_out = nrm(ks[14], (DEPTH, D_MODEL, D_MODEL), D_MODEL ** -0.5)
    w_gate = nrm(ks[15], (DEPTH, D_MODEL, D_FF), D_MODEL ** -0.5)
    w_up = nrm(ks[16], (DEPTH, D_MODEL, D_FF), D_MODEL ** -0.5)
    w_down = nrm(ks[17], (DEPTH, D_FF, D_MODEL), D_FF ** -0.5)
    return {'x': x, 'c': c, 'positions': positions, 'ada_w': ada_w, 'ada_b': ada_b,
            'norm1_g': norm1_g, 'norm2_g': norm2_g, 'normf_g': normf_g, 'w_in': w_in, 'b_in': b_in,
            'conv_w': conv_w, 'conv_b': conv_b, 'm_norm_g': m_norm_g, 'p_mlstm': p_mlstm,
            'p_moba': p_moba, 'w_out': w_out, 'w_gate': w_gate, 'w_up': w_up, 'w_down': w_down}


def reference(x, c, positions, ada_w, ada_b, norm1_g, norm2_g, normf_g, w_in, b_in,
              conv_w, conv_b, m_norm_g, p_mlstm, p_moba, w_out, w_gate, w_up, w_down):
    B, S, D = x.shape
    split_points = np.cumsum(IN_SPLITS)[:-1].tolist()
    c_act = jax.nn.silu(c)
    for l in range(DEPTH):
        mod = jnp.dot(c_act, ada_w[l]) + ada_b[l]
        sh1, sc1, g1, sh2, sc2, g2 = jnp.split(mod, N_MOD, axis=-1)

        h = modulate(rms_norm(x, norm1_g[l]), sh1, sc1)
        z = jnp.einsum('bsd,de->bse', h, w_in[l]) + b_in[l]
        mq, mk, mv, mo, mi, mf, aq, ak, av, ga, gb = jnp.split(z, split_points, axis=-1)

        qk = jax.nn.silu(causal_dwconv(jnp.concatenate([mq, mk], axis=-1), conv_w[l], conv_b[l]))
        mq, mk = jnp.split(qk, 2, axis=-1)
        hm = mlstm_chunkwise(split_heads(mq, M_HEADS, M_HEAD_DIM), split_heads(mk, M_HEADS, M_HEAD_DIM),
                             split_heads(mv, M_HEADS, M_HEAD_DIM), mi.transpose(0, 2, 1), mf.transpose(0, 2, 1))
        hm = rms_norm(hm.transpose(0, 2, 1, 3), m_norm_g[l].reshape(M_HEADS, M_HEAD_DIM))
        ym = (hm.reshape(B, S, M_WIDTH) * jax.nn.sigmoid(mo)).astype(x.dtype)

        qa = rope(split_heads(aq, A_HEADS, A_HEAD_DIM), positions)
        ka = rope(split_heads(ak, A_HEADS, A_HEAD_DIM), positions)
        va = split_heads(av, A_HEADS, A_HEAD_DIM)
        ya = merge_heads(moba_attention(qa, ka, va))

        merged = (jax.nn.sigmoid(ga) * jnp.einsum('bsm,md->bsd', ym, p_mlstm[l])
                  + jax.nn.sigmoid(gb) * jnp.einsum('bsa,ad->bsd', ya, p_moba[l]))
        x = x + g1[:, None, :] * jnp.einsum('bsd,de->bse', merged, w_out[l])

        h = modulate(rms_norm(x, norm2_g[l]), sh2, sc2)
        f = jax.nn.silu(jnp.einsum('bsd,df->bsf', h, w_gate[l])) * jnp.einsum('bsd,df->bsf', h, w_up[l])
        x = x + g2[:, None, :] * jnp.einsum('bsf,fd->bsd', f, w_down[l])
    return rms_norm(x, normf_g)
```

```python
import functools

import jax
import jax.numpy as jnp
from jax import lax
from jax.experimental import pallas as pl
from jax.experimental.pallas import tpu as pltpu

F32 = jnp.float32
BF16 = jnp.bfloat16

M_HEADS = 4
M_HEAD_DIM = 128
M_WIDTH = M_HEADS * M_HEAD_DIM
M_CONV = 4
A_HEADS = 8
A_HEAD_DIM = 64
A_WIDTH = A_HEADS * A_HEAD_DIM
MOBA_BLOCK = 256
MOBA_TOPK = 3
ROPE_THETA = 10000.0
N_MOD = 6
RMS_EPS = 1e-6

LANES = 128
SUBLANES = 8
VMEM_LIMIT = 56 * 1024 * 1024

ROW_TILE = 512
MLSTM_CHUNK = 256
NEG_BIG = -1e30

OFF_MQ, OFF_MK, OFF_MV, OFF_MO = 0, 512, 1024, 1536
OFF_AQ, OFF_AK, OFF_AV = 2048, 2560, 3072
OFF_GA, OFF_GB = 3584, 4608
OFF_IF = 5632
D_IN_PACKED = 5760


def _cparams(sem):
    return pltpu.CompilerParams(dimension_semantics=sem, vmem_limit_bytes=VMEM_LIMIT)


def _const_spec(shape):
    nd = len(shape)
    return pl.BlockSpec(shape, lambda *_: (0,) * nd, pipeline_mode=pl.Buffered(1))


def _rms(x, g):
    ms = jnp.mean(x * x, axis=-1, keepdims=True)
    return x * lax.rsqrt(ms + RMS_EPS) * g


def _dot(a, b):
    return jnp.dot(a, b, preferred_element_type=F32)


def _dot_nt(a, b):
    return lax.dot_general(a, b, (((1,), (1,)), ((), ())), preferred_element_type=F32)


def _ada_kernel(c_ref, w_ref, b_ref, o_ref):
    c = c_ref[...]
    ca = c * jax.nn.sigmoid(c)
    o_ref[...] = jnp.dot(ca, w_ref[...], precision=lax.Precision.HIGHEST,
                         preferred_element_type=F32) + b_ref[...]


def _ada_mod(c, ada_w, ada_b):
    bsz, d = c.shape
    n = ada_w.shape[1]
    tn = d
    return pl.pallas_call(
        _ada_kernel,
        grid=(n // tn,),
        in_specs=[pl.BlockSpec((bsz, d), lambda j: (0, 0)),
                  pl.BlockSpec((d, tn), lambda j: (0, j)),
                  pl.BlockSpec((1, tn), lambda j: (0, j))],
        out_specs=pl.BlockSpec((bsz, tn), lambda j: (0, j)),
        out_shape=jax.ShapeDtypeStruct((bsz, n), F32),
        compiler_params=_cparams(("arbitrary",)),
        name="ada_mod",
    )(c, ada_w, ada_b.reshape(1, n))


def _rope_kernel(pos_ref, freq_ref, cos_ref, sin_ref):
    pos = pos_ref[0].astype(F32)
    ang = pos * freq_ref[...]
    lane = lax.broadcasted_iota(jnp.int32, ang.shape, 1)
    first_half = (lane % A_HEAD_DIM) < (A_HEAD_DIM // 2)
    s = jnp.sin(ang)
    cos_ref[0] = jnp.cos(ang)
    sin_ref[0] = jnp.where(first_half, -s, s)


def _rope_tables(positions):
    bsz, seq = positions.shape
    half = A_HEAD_DIM // 2
    inv_freq = ROPE_THETA ** (-jnp.arange(half, dtype=F32) / half)
    freq = jnp.tile(inv_freq, LANES // half).reshape(1, LANES)
    ts = 2048
    out = jax.ShapeDtypeStruct((bsz, seq, LANES), F32)
    return pl.pallas_call(
        _rope_kernel,
        grid=(bsz, seq // ts),
        in_specs=[pl.BlockSpec((1, ts, 1), lambda b, i: (b, i, 0)),
                  pl.BlockSpec((1, LANES), lambda b, i: (0, 0))],
        out_specs=[pl.BlockSpec((1, ts, LANES), lambda b, i: (b, i, 0))] * 2,
        out_shape=[out, out],
        compiler_params=_cparams(("arbitrary", "arbitrary")),
        name="rope_table",
    )(positions.reshape(bsz, seq, 1), freq)


def _in_proj_kernel(x_ref, mod_ref, g_ref, w_ref, b_ref, cw_ref, cb_ref, cos_ref, sin_ref,
                    qm_ref, km_ref, vm_ref, om_ref, qa_ref, ka_ref, va_ref,
                    sga_ref, sgb_ref, zif_ref, kmean_ref, hb_ref, zbuf_ref):
    tm = x_ref.shape[1]
    it = pl.program_id(1)

    x = x_ref[0]
    sh = mod_ref[0, 0:1, :]
    sc = mod_ref[0, 1:2, :]
    hb_ref[...] = (_rms(x, g_ref[...]) * (1.0 + sc) + sh).astype(BF16)

    def proj(off, width):
        return _dot(hb_ref[...], w_ref[:, off:off + width]) + b_ref[:, off:off + width]

    @pl.when(it == 0)
    def _():
        zbuf_ref[0:SUBLANES, :] = jnp.zeros((SUBLANES, zbuf_ref.shape[1]), F32)

    @pl.when(it > 0)
    def _():
        zbuf_ref[0:SUBLANES, :] = zbuf_ref[tm:tm + SUBLANES, :]

    for half, (off, out_ref, scale) in enumerate(((OFF_MQ, qm_ref, 1.0),
                                                  (OFF_MK, km_ref, M_HEAD_DIM ** -0.5))):
        cols = slice(half * M_WIDTH, (half + 1) * M_WIDTH)
        zbuf_ref[SUBLANES:SUBLANES + tm, cols] = proj(off, M_WIDTH)
        acc = jnp.broadcast_to(cb_ref[:, cols], (tm, M_WIDTH))
        for j in range(M_CONV):
            start = SUBLANES - (M_CONV - 1) + j
            acc = acc + zbuf_ref[start:start + tm, cols] * cw_ref[j:j + 1, cols]
        y = acc * jax.nn.sigmoid(acc)
        out_ref[0] = (y * scale).astype(BF16)

    vm_ref[0] = proj(OFF_MV, M_WIDTH).astype(BF16)
    om_ref[0] = proj(OFF_MO, M_WIDTH).astype(BF16)

    def rope(z):
        outs = []
        for p in range(A_WIDTH // LANES):
            zp = z[:, p * LANES:(p + 1) * LANES]
            lane = lax.broadcasted_iota(jnp.int32, zp.shape, 1)
            first_half = (lane % A_HEAD_DIM) < (A_HEAD_DIM // 2)
            rot = jnp.where(first_half,
                            pltpu.roll(zp, LANES - A_HEAD_DIM // 2, 1),
                            pltpu.roll(zp, A_HEAD_DIM // 2, 1))
            outs.append(zp * cos_ref[0] + rot * sin_ref[0])
        return outs

    q_parts = rope(proj(OFF_AQ, A_WIDTH))
    for p, qp in enumerate(q_parts):
        qa_ref[0, :, p * LANES:(p + 1) * LANES] = (qp * (A_HEAD_DIM ** -0.5)).astype(BF16)

    k_parts = rope(proj(OFF_AK, A_WIDTH))
    kmean_ref[0] = jnp.zeros(kmean_ref.shape[1:], F32)
    for p, kp in enumerate(k_parts):
        ka_ref[0, :, p * LANES:(p + 1) * LANES] = kp.astype(BF16)
        for blk in range(tm // MOBA_BLOCK):
            kmean_ref[0, blk:blk + 1, p * LANES:(p + 1) * LANES] = jnp.mean(
                kp[blk * MOBA_BLOCK:(blk + 1) * MOBA_BLOCK], axis=0, keepdims=True)

    va_ref[0] = proj(OFF_AV, A_WIDTH).astype(BF16)

    d = x.shape[1]
    for off, out_ref in ((OFF_GA, sga_ref), (OFF_GB, sgb_ref)):
        for c0 in range(0, d, 512):
            out_ref[0, :, c0:c0 + 512] = jax.nn.sigmoid(proj(off + c0, 512)).astype(BF16)

    zif_ref[0] = proj(OFF_IF, LANES)


def _in_proj(x, mod3, norm_g, w_packed, b_packed, conv_w, conv_b, cos_t, sin_t):
    bsz, seq, d = x.shape
    tm = ROW_TILE
    nt = seq // tm
    row = lambda b, i: (b, i, 0)

    def rows(width, dtype):
        return (pl.BlockSpec((1, tm, width), row), jax.ShapeDtypeStruct((bsz, seq, width), dtype))

    outs = [rows(M_WIDTH, BF16)] * 4 + [rows(A_WIDTH, BF16)] * 3 + [rows(d, BF16)] * 2 + [rows(LANES, F32)]
    outs.append((pl.BlockSpec((1, SUBLANES, A_WIDTH), lambda b, i: (b * nt + i, 0, 0)),
                 jax.ShapeDtypeStruct((bsz * nt, SUBLANES, A_WIDTH), F32)))
    return pl.pallas_call(
        _in_proj_kernel,
        grid=(bsz, nt),
        in_specs=[pl.BlockSpec((1, tm, d), row),
                  pl.BlockSpec((1, N_MOD, d), lambda b, i: (b, 0, 0)),
                  _const_spec((1, d)),
                  _const_spec(w_packed.shape),
                  _const_spec(b_packed.shape),
                  _const_spec(conv_w.shape),
                  _const_spec(conv_b.shape),
                  pl.BlockSpec((1, tm, LANES), row),
                  pl.BlockSpec((1, tm, LANES), row)],
        out_specs=[o[0] for o in outs],
        out_shape=[o[1] for o in outs],
        scratch_shapes=[pltpu.VMEM((tm, d), BF16),
                        pltpu.VMEM((tm + SUBLANES, 2 * M_WIDTH), F32)],
        compiler_params=_cparams(("arbitrary", "arbitrary")),
        name="in_proj",
    )(x, mod3, norm_g, w_packed, b_packed, conv_w, conv_b, cos_t, sin_t)


def _split3(x):
    hi = x.astype(BF16)
    r1 = x - hi.astype(F32)
    mid = r1.astype(BF16)
    lo = (r1 - mid.astype(F32)).astype(BF16)
    return hi, mid, lo


def _mlstm_kernel(q_ref, k_ref, v_ref, o_ref, zif_ref, g_ref, ym_ref, ct_ref, m_ref):
    L = q_ref.shape[1]
    dh = M_HEAD_DIM

    @pl.when(pl.program_id(1) == 0)
    def _():
        ct_ref[...] = jnp.zeros(ct_ref.shape, F32)
        m_ref[...] = jnp.zeros(m_ref.shape, F32)

    zif = zif_ref[0]
    lane = lax.broadcasted_iota(jnp.int32, zif.shape, 1)
    log_f = jnp.minimum(zif, 0.0) - jnp.log1p(jnp.exp(-jnp.abs(zif)))
    log_f = jnp.where((lane >= M_HEADS) & (lane < 2 * M_HEADS), log_f, 0.0)
    r_i = lax.broadcasted_iota(jnp.int32, (L, L), 0)
    c_i = lax.broadcasted_iota(jnp.int32, (L, L), 1)
    causal = c_i <= r_i
    tri = jnp.where(causal, 1.0, 0.0).astype(BF16)
    bcum = sum(_dot(tri, part) for part in _split3(log_f))
    xcol = jnp.where(lane < M_HEADS, zif, bcum)
    xrow = xcol.T
    ones = jnp.ones((L, dh), BF16)

    for h in range(M_HEADS):
        hs = slice(h * dh, (h + 1) * dh)
        li_col = xcol[:, h:h + 1]
        b_col = xcol[:, M_HEADS + h:M_HEADS + h + 1]
        li_row = xrow[h:h + 1, :]
        b_row = xrow[M_HEADS + h:M_HEADS + h + 1, :]
        m_prev = m_ref[h:h + 1, 0:1]
        b_last = b_col[L - 1:L, :]
        q = q_ref[0, :, hs]
        k = k_ref[0, :, hs]
        vaug = jnp.concatenate([v_ref[0, :, hs], ones], axis=1)
        ct = ct_ref[h]

        s = _dot_nt(q, k)
        dmat = jnp.where(causal, b_col - b_row + li_row, -jnp.inf)
        inter = b_col + m_prev
        m_t = jnp.maximum(inter, jnp.max(dmat, axis=1, keepdims=True))
        sw = (s * jnp.exp(dmat - m_t)).astype(BF16)
        nd = jnp.exp(inter - m_t) * _dot(q, ct.astype(BF16)) + _dot(sw, vaug)
        hval = nd[:, :dh] / jnp.maximum(jnp.abs(nd[:, dh:]), jnp.exp(-m_t))
        hn = _rms(hval, g_ref[:, hs])
        ym_ref[0, :, hs] = (hn * jax.nn.sigmoid(o_ref[0, :, hs].astype(F32))).astype(BF16)

        a_col = b_last - b_col + li_col
        m_new = jnp.maximum(b_last + m_prev, jnp.max(a_col, axis=0, keepdims=True))
        kw_t = (k.astype(F32) * jnp.exp(a_col - m_new)).T.astype(BF16)
        ct_ref[h] = jnp.exp(b_last + m_prev - m_new) * ct + _dot(kw_t, vaug)
        m_ref[h:h + 1, :] = jnp.broadcast_to(m_new, (1, LANES))


def _mlstm(qm, km, vm, om, zif, m_norm_g):
    bsz, seq, width = qm.shape
    L = MLSTM_CHUNK
    row = lambda b, c: (b, c, 0)
    return pl.pallas_call(
        _mlstm_kernel,
        grid=(bsz, seq // L),
        in_specs=[pl.BlockSpec((1, L, width), row)] * 4
        + [pl.BlockSpec((1, L, LANES), row), _const_spec((1, width))],
        out_specs=pl.BlockSpec((1, L, width), row),
        out_shape=jax.ShapeDtypeStruct((bsz, seq, width), BF16),
        scratch_shapes=[pltpu.VMEM((M_HEADS, M_HEAD_DIM, 2 * M_HEAD_DIM), F32),
                        pltpu.VMEM((SUBLANES, LANES), F32)],
        compiler_params=_cparams(("arbitrary", "arbitrary")),
        name="mlstm",
    )(qm, km, vm, om, zif, m_norm_g)


def _moba_kernel(q_ref, k_ref, v_ref, kmean_ref, o_ref):
    tq = q_ref.shape[1]
    blk = MOBA_BLOCK
    i = pl.program_id(2)
    qp = q_ref[0]
    lane = lax.broadcasted_iota(jnp.int32, (tq, LANES), 1)
    kmean = kmean_ref[0]

    q_aug = []
    q_own = []
    for hh in range(2):
        in_head = (lane >= hh * A_HEAD_DIM) & (lane < (hh + 1) * A_HEAD_DIM)
        qh = jnp.where(in_head, qp.astype(F32), 0.0).astype(BF16)
        gate = _dot_nt(qh, kmean)
        valid = lane < i
        g = jnp.where(valid, gate, -jnp.inf)
        sel = jnp.zeros(g.shape, jnp.bool_)
        for _ in range(MOBA_TOPK):
            mx = jnp.max(g, axis=1, keepdims=True)
            idx = jnp.min(jnp.where(g == mx, lane, LANES), axis=1, keepdims=True)
            pick = (lane == idx) & valid
            sel = sel | pick
            g = jnp.where(pick, -jnp.inf, g)
        bias = jnp.where(sel, 0.0, NEG_BIG).astype(BF16)
        q_own.append(qh)
        q_aug.append(jnp.concatenate([qh, bias], axis=1))
    q_own = jnp.concatenate(q_own, axis=0)
    q_aug = jnp.concatenate(q_aug, axis=0)

    start = pl.multiple_of(i * blk, blk)
    k_own = k_ref[0, pl.ds(start, blk), :]
    v_own = v_ref[0, pl.ds(start, blk), :]
    s = _dot_nt(q_own, k_own)
    r_i = lax.broadcasted_iota(jnp.int32, (2 * tq, blk), 0) % tq
    c_i = lax.broadcasted_iota(jnp.int32, (2 * tq, blk), 1)
    s = jnp.where(c_i <= r_i, s, NEG_BIG)
    m0 = jnp.max(s, axis=1, keepdims=True)
    p = jnp.exp(s - m0)
    l0 = jnp.sum(p, axis=1, keepdims=True)
    acc0 = _dot(p.astype(BF16), v_own)

    lane_k = lax.broadcasted_iota(jnp.int32, (blk, LANES), 1)

    def body(j, carry):
        m, l, acc = carry
        st = pl.multiple_of(j * blk, blk)
        kj = k_ref[0, pl.ds(st, blk), :]
        vj = v_ref[0, pl.ds(st, blk), :]
        onehot = jnp.where(lane_k == j, 1.0, 0.0).astype(BF16)
        k_aug = jnp.concatenate([kj, onehot], axis=1)
        s = _dot_nt(q_aug, k_aug)
        m_new = jnp.maximum(m, jnp.max(s, axis=1, keepdims=True))
        alpha = jnp.exp(m - m_new)
        p = jnp.exp(s - m_new)
        l = alpha * l + jnp.sum(p, axis=1, keepdims=True)
        acc = alpha * acc + _dot(p.astype(BF16), vj)
        return m_new, l, acc

    _, l, acc = lax.fori_loop(0, i, body, (m0, l0, acc0))
    out = acc / l
    o_ref[0] = jnp.where(lane < A_HEAD_DIM, out[:tq], out[tq:]).astype(BF16)


def _moba(qa, ka, va, kmean_pad):
    bsz, seq, width = qa.shape
    tq = MOBA_BLOCK
    npair = width // LANES
    qmap = lambda b, p, i: (b, i, p)
    kvmap = lambda b, p, i: (b, 0, p)
    return pl.pallas_call(
        _moba_kernel,
        grid=(bsz, npair, seq // tq),
        in_specs=[pl.BlockSpec((1, tq, LANES), qmap),
                  pl.BlockSpec((1, seq, LANES), kvmap),
                  pl.BlockSpec((1, seq, LANES), kvmap),
                  pl.BlockSpec((1, LANES, LANES), kvmap)],
        out_specs=pl.BlockSpec((1, tq, LANES), qmap),
        out_shape=jax.ShapeDtypeStruct((bsz, seq, width), BF16),
        compiler_params=_cparams(("arbitrary", "arbitrary", "arbitrary")),
        name="moba",
    )(qa, ka, va, kmean_pad)


def _merge_kernel(x_ref, ym_ref, ya_ref, sga_ref, sgb_ref, mod_ref, pm_ref, pa_ref, wo_ref, o_ref):
    merged = (sga_ref[0].astype(F32) * _dot(ym_ref[0], pm_ref[...])
              + sgb_ref[0].astype(F32) * _dot(ya_ref[0], pa_ref[...]))
    g1 = mod_ref[0, 2:3, :]
    o_ref[0] = x_ref[0] + g1 * _dot(merged.astype(BF16), wo_ref[...])


def _merge_out(x, ym, ya, sga, sgb, mod3, p_m, p_a, w_out):
    bsz, seq, d = x.shape
    tm = ROW_TILE
    row = lambda b, i: (b, i, 0)
    return pl.pallas_call(
        _merge_kernel,
        grid=(bsz, seq // tm),
        in_specs=[pl.BlockSpec((1, tm, d), row),
                  pl.BlockSpec((1, tm, ym.shape[2]), row),
                  pl.BlockSpec((1, tm, ya.shape[2]), row),
                  pl.BlockSpec((1, tm, d), row),
                  pl.BlockSpec((1, tm, d), row),
                  pl.BlockSpec((1, N_MOD, d), lambda b, i: (b, 0, 0)),
                  _const_spec(p_m.shape), _const_spec(p_a.shape), _const_spec(w_out.shape)],
        out_specs=pl.BlockSpec((1, tm, d), row),
        out_shape=jax.ShapeDtypeStruct((bsz, seq, d), F32),
        compiler_params=_cparams(("arbitrary", "arbitrary")),
        name="merge_out",
    )(x, ym, ya, sga, sgb, mod3, p_m, p_a, w_out)


def _ffn_kernel(x_ref, mod_ref, g2_ref, gf_ref, wg_ref, wu_ref, wd_ref, o_ref, *, final_norm):
    x = x_ref[0]
    sh = mod_ref[0, 3:4, :]
    sc = mod_ref[0, 4:5, :]
    gate = mod_ref[0, 5:6, :]
    h = (_rms(x, g2_ref[...]) * (1.0 + sc) + sh).astype(BF16)
    d_ff = wg_ref.shape[1]
    step = 256
    acc = jnp.zeros(x.shape, F32)
    for c0 in range(0, d_ff, step):
        a = _dot(h, wg_ref[:, c0:c0 + step])
        u = _dot(h, wu_ref[:, c0:c0 + step])
        f = (a * jax.nn.sigmoid(a) * u).astype(BF16)
        acc = acc + _dot(f, wd_ref[c0:c0 + step, :])
    y = x + gate * acc
    o_ref[0] = _rms(y, gf_ref[...]) if final_norm else y


def _ffn_out(x, mod3, norm2_g, normf_g, w_gate, w_up, w_down, final_norm):
    bsz, seq, d = x.shape
    tm = ROW_TILE
    row = lambda b, i: (b, i, 0)
    return pl.pallas_call(
        functools.partial(_ffn_kernel, final_norm=final_norm),
        grid=(bsz, seq // tm),
        in_specs=[pl.BlockSpec((1, tm, d), row),
                  pl.BlockSpec((1, N_MOD, d), lambda b, i: (b, 0, 0)),
                  _const_spec((1, d)), _const_spec((1, d)),
                  _const_spec(w_gate.shape), _const_spec(w_up.shape), _const_spec(w_down.shape)],
        out_specs=pl.BlockSpec((1, tm, d), row),
        out_shape=jax.ShapeDtypeStruct((bsz, seq, d), F32),
        compiler_params=_cparams(("arbitrary", "arbitrary")),
        name="ffn_out",
    )(x, mod3, norm2_g, normf_g, w_gate, w_up, w_down)


def _pack_in_proj(w_in, b_in):
    m_end = 4 * M_WIDTH
    if_end = m_end + 2 * M_HEADS
    pad = LANES - 2 * M_HEADS

    def pack(t):
        return jnp.concatenate(
            [t[..., :m_end], t[..., if_end:], t[..., m_end:if_end],
             jnp.zeros(t.shape[:-1] + (pad,), t.dtype)], axis=-1)

    return pack(w_in).astype(BF16), pack(b_in.reshape(1, -1))


def kernel(x, c, positions, ada_w, ada_b, norm1_g, norm2_g, normf_g, w_in, b_in, conv_w, conv_b,
           m_norm_g, p_mlstm, p_moba, w_out, w_gate, w_up, w_down):
    bsz, seq, d = x.shape
    depth = ada_w.shape[0]
    n_blocks = seq // MOBA_BLOCK
    cos_t, sin_t = _rope_tables(positions)
    for l in range(depth):
        mod3 = _ada_mod(c, ada_w[l], ada_b[l]).reshape(bsz, N_MOD, d)
        w_packed, b_packed = _pack_in_proj(w_in[l], b_in[l])
        (qm, km, vm, om, qa, ka, va, sga, sgb, zif, kmean_tiles) = _in_proj(
            x, mod3, norm1_g[l].reshape(1, d), w_packed, b_packed, conv_w[l],
            conv_b[l].reshape(1, -1), cos_t, sin_t)
        ym = _mlstm(qm, km, vm, om, zif, m_norm_g[l].reshape(1, -1))
        kmean = kmean_tiles[:, :ROW_TILE // MOBA_BLOCK].reshape(bsz, n_blocks, A_WIDTH)
        kmean_pad = jnp.pad(kmean, ((0, 0), (0, LANES - n_blocks), (0, 0))).astype(BF16)
        ya = _moba(qa, ka, va, kmean_pad)
        x = _merge_out(x, ym, ya, sga, sgb, mod3, p_mlstm[l].astype(BF16), p_moba[l].astype(BF16),
                       w_out[l].astype(BF16))
        x = _ffn_out(x, mod3, norm2_g[l].reshape(1, d), normf_g.reshape(1, d),
                     w_gate[l].astype(BF16), w_up[l].astype(BF16), w_down[l].astype(BF16),
                     final_norm=(l == depth - 1))
    return x
```

```python
import functools

import jax
import jax.numpy as jnp
from jax import lax
from jax.experimental import pallas as pl
from jax.experimental.pallas import tpu as pltpu

F32 = jnp.float32
BF16 = jnp.bfloat16

M_HEADS = 4
M_HEAD_DIM = 128
M_WIDTH = M_HEADS * M_HEAD_DIM
M_CONV = 4
A_HEADS = 8
A_HEAD_DIM = 64
A_WIDTH = A_HEADS * A_HEAD_DIM
MOBA_BLOCK = 256
MOBA_TOPK = 3
ROPE_THETA = 10000.0
N_MOD = 6
RMS_EPS = 1e-6

LANES = 128
SUBLANES = 8
VMEM_LIMIT = 56 * 1024 * 1024

ROW_TILE = 512
MLSTM_CHUNK = 256
NEG_BIG = -1e30
LOG2_E = 1.4426950408889634
ACC_ROWS = A_HEAD_DIM + 16

OFF_MQ, OFF_MK, OFF_MV, OFF_MO = 0, 512, 1024, 1536
OFF_AQ, OFF_AK, OFF_AV = 2048, 2560, 3072
OFF_GA, OFF_GB = 3584, 4608
OFF_IF = 5632
D_IN_PACKED = 5760


def _cparams(sem, flags=None):
    return pltpu.CompilerParams(dimension_semantics=sem, vmem_limit_bytes=VMEM_LIMIT, flags=flags)


def _const_spec(shape):
    nd = len(shape)
    return pl.BlockSpec(shape, lambda *_: (0,) * nd, pipeline_mode=pl.Buffered(1))


def _rms(x, g):
    ms = jnp.mean(x * x, axis=-1, keepdims=True)
    return x * lax.rsqrt(ms + RMS_EPS) * g


def _dot(a, b):
    return jnp.dot(a, b, preferred_element_type=F32)


def _dot_nt(a, b):
    return lax.dot_general(a, b, (((1,), (1,)), ((), ())), preferred_element_type=F32)


def _ada_kernel(c_ref, w_ref, b_ref, o_ref):
    c = c_ref[...]
    ca = c * jax.nn.sigmoid(c)
    o_ref[...] = jnp.dot(ca, w_ref[...], precision=lax.Precision.HIGHEST,
                         preferred_element_type=F32) + b_ref[...]


def _ada_mod(c, ada_w, ada_b):
    bsz, d = c.shape
    n = ada_w.shape[1]
    tn = d
    return pl.pallas_call(
        _ada_kernel,
        grid=(n // tn,),
        in_specs=[pl.BlockSpec((bsz, d), lambda j: (0, 0)),
                  pl.BlockSpec((d, tn), lambda j: (0, j)),
                  pl.BlockSpec((1, tn), lambda j: (0, j))],
        out_specs=pl.BlockSpec((bsz, tn), lambda j: (0, j)),
        out_shape=jax.ShapeDtypeStruct((bsz, n), F32),
        compiler_params=_cparams(("arbitrary",)),
        name="ada_mod",
    )(c, ada_w, ada_b.reshape(1, n))


def _rope_kernel(pos_ref, freq_ref, cos_ref, sin_ref):
    pos = pos_ref[0].astype(F32)
    ang = pos * freq_ref[...]
    lane = lax.broadcasted_iota(jnp.int32, ang.shape, 1)
    first_half = (lane % A_HEAD_DIM) < (A_HEAD_DIM // 2)
    s = jnp.sin(ang)
    cos_ref[0] = jnp.cos(ang)
    sin_ref[0] = jnp.where(first_half, -s, s)


def _rope_tables(positions):
    bsz, seq = positions.shape
    half = A_HEAD_DIM // 2
    inv_freq = ROPE_THETA ** (-jnp.arange(half, dtype=F32) / half)
    freq = jnp.tile(inv_freq, LANES // half).reshape(1, LANES)
    ts = 2048
    out = jax.ShapeDtypeStruct((bsz, seq, LANES), F32)
    return pl.pallas_call(
        _rope_kernel,
        grid=(bsz, seq // ts),
        in_specs=[pl.BlockSpec((1, ts, 1), lambda b, i: (b, i, 0)),
                  pl.BlockSpec((1, LANES), lambda b, i: (0, 0))],
        out_specs=[pl.BlockSpec((1, ts, LANES), lambda b, i: (b, i, 0))] * 2,
        out_shape=[out, out],
        compiler_params=_cparams(("arbitrary", "arbitrary")),
        name="rope_table",
    )(positions.reshape(bsz, seq, 1), freq)


def _in_proj_kernel(x_ref, mod_ref, g_ref, w_ref, b_ref, cw_ref, cb_ref, cos_ref, sin_ref,
                    qm_ref, km_ref, vm_ref, om_ref, qa_ref, ka_ref, va_ref,
                    sga_ref, sgb_ref, zif_ref, kmean_ref, hb_ref, zbuf_ref):
    tm = x_ref.shape[1]
    it = pl.program_id(1)

    x = x_ref[0]
    sh = mod_ref[0, 0:1, :]
    sc = mod_ref[0, 1:2, :]
    hb_ref[...] = (_rms(x, g_ref[...]) * (1.0 + sc) + sh).astype(BF16)

    def proj(off, width):
        return _dot(hb_ref[...], w_ref[:, off:off + width]) + b_ref[:, off:off + width]

    @pl.when(it == 0)
    def _():
        zbuf_ref[0:SUBLANES, :] = jnp.zeros((SUBLANES, zbuf_ref.shape[1]), F32)

    @pl.when(it > 0)
    def _():
        zbuf_ref[0:SUBLANES, :] = zbuf_ref[tm:tm + SUBLANES, :]

    for half, (off, out_ref, scale) in enumerate(((OFF_MQ, qm_ref, 1.0),
                                                  (OFF_MK, km_ref, M_HEAD_DIM ** -0.5))):
        cols = slice(half * M_WIDTH, (half + 1) * M_WIDTH)
        zbuf_ref[SUBLANES:SUBLANES + tm, cols] = proj(off, M_WIDTH)
        acc = jnp.broadcast_to(cb_ref[:, cols], (tm, M_WIDTH))
        for j in range(M_CONV):
            start = SUBLANES - (M_CONV - 1) + j
            acc = acc + zbuf_ref[start:start + tm, cols] * cw_ref[j:j + 1, cols]
        y = acc * jax.nn.sigmoid(acc)
        out_ref[0] = (y * scale).astype(BF16)

    vm_ref[0] = proj(OFF_MV, M_WIDTH).astype(BF16)
    om_ref[0] = proj(OFF_MO, M_WIDTH).astype(BF16)

    def rope(z):
        outs = []
        for p in range(A_WIDTH // LANES):
            zp = z[:, p * LANES:(p + 1) * LANES]
            lane = lax.broadcasted_iota(jnp.int32, zp.shape, 1)
            first_half = (lane % A_HEAD_DIM) < (A_HEAD_DIM // 2)
            rot = jnp.where(first_half,
                            pltpu.roll(zp, LANES - A_HEAD_DIM // 2, 1),
                            pltpu.roll(zp, A_HEAD_DIM // 2, 1))
            outs.append(zp * cos_ref[0] + rot * sin_ref[0])
        return outs

    q_parts = rope(proj(OFF_AQ, A_WIDTH))
    for p, qp in enumerate(q_parts):
        qa_ref[0, p * LANES:(p + 1) * LANES, :] = (qp * (LOG2_E * A_HEAD_DIM ** -0.5)).T.astype(BF16)

    k_parts = rope(proj(OFF_AK, A_WIDTH))
    kmean_ref[0] = jnp.zeros(kmean_ref.shape[1:], F32)
    for p, kp in enumerate(k_parts):
        ka_ref[0, :, p * LANES:(p + 1) * LANES] = kp.astype(BF16)
        for blk in range(tm // MOBA_BLOCK):
            kmean_ref[0, blk:blk + 1, p * LANES:(p + 1) * LANES] = jnp.mean(
                kp[blk * MOBA_BLOCK:(blk + 1) * MOBA_BLOCK], axis=0, keepdims=True)

    va = proj(OFF_AV, A_WIDTH)
    for p in range(A_WIDTH // LANES):
        va_ref[0, p * LANES:(p + 1) * LANES, :] = va[:, p * LANES:(p + 1) * LANES].T.astype(BF16)

    d = x.shape[1]
    for off, out_ref in ((OFF_GA, sga_ref), (OFF_GB, sgb_ref)):
        for c0 in range(0, d, 512):
            out_ref[0, :, c0:c0 + 512] = jax.nn.sigmoid(proj(off + c0, 512)).astype(BF16)

    zif_ref[0] = proj(OFF_IF, LANES)


def _in_proj(x, mod3, norm_g, w_packed, b_packed, conv_w, conv_b, cos_t, sin_t):
    bsz, seq, d = x.shape
    tm = ROW_TILE
    nt = seq // tm
    row = lambda b, i: (b, i, 0)

    def rows(width, dtype):
        return (pl.BlockSpec((1, tm, width), row), jax.ShapeDtypeStruct((bsz, seq, width), dtype))

    cols = (pl.BlockSpec((1, A_WIDTH, tm), lambda b, i: (b, 0, i)),
            jax.ShapeDtypeStruct((bsz, A_WIDTH, seq), BF16))
    outs = ([rows(M_WIDTH, BF16)] * 4 + [cols, rows(A_WIDTH, BF16), cols]
            + [rows(d, BF16)] * 2 + [rows(LANES, F32)])
    outs.append((pl.BlockSpec((1, SUBLANES, A_WIDTH), lambda b, i: (b * nt + i, 0, 0)),
                 jax.ShapeDtypeStruct((bsz * nt, SUBLANES, A_WIDTH), F32)))
    return pl.pallas_call(
        _in_proj_kernel,
        grid=(bsz, nt),
        in_specs=[pl.BlockSpec((1, tm, d), row),
                  pl.BlockSpec((1, N_MOD, d), lambda b, i: (b, 0, 0)),
                  _const_spec((1, d)),
                  _const_spec(w_packed.shape),
                  _const_spec(b_packed.shape),
                  _const_spec(conv_w.shape),
                  _const_spec(conv_b.shape),
                  pl.BlockSpec((1, tm, LANES), row),
                  pl.BlockSpec((1, tm, LANES), row)],
        out_specs=[o[0] for o in outs],
        out_shape=[o[1] for o in outs],
        scratch_shapes=[pltpu.VMEM((tm, d), BF16),
                        pltpu.VMEM((tm + SUBLANES, 2 * M_WIDTH), F32)],
        compiler_params=_cparams(("arbitrary", "arbitrary")),
        name="in_proj",
    )(x, mod3, norm_g, w_packed, b_packed, conv_w, conv_b, cos_t, sin_t)


def _split3(x):
    hi = x.astype(BF16)
    r1 = x - hi.astype(F32)
    mid = r1.astype(BF16)
    lo = (r1 - mid.astype(F32)).astype(BF16)
    return hi, mid, lo


def _mlstm_kernel(q_ref, k_ref, v_ref, o_ref, zif_ref, g_ref, ym_ref, ct_ref, m_ref):
    L = q_ref.shape[1]
    dh = M_HEAD_DIM

    @pl.when(pl.program_id(1) == 0)
    def _():
        ct_ref[...] = jnp.zeros(ct_ref.shape, F32)
        m_ref[...] = jnp.zeros(m_ref.shape, F32)

    zif = zif_ref[0]
    lane = lax.broadcasted_iota(jnp.int32, zif.shape, 1)
    log_f = jnp.minimum(zif, 0.0) - jnp.log1p(jnp.exp(-jnp.abs(zif)))
    log_f = jnp.where((lane >= M_HEADS) & (lane < 2 * M_HEADS), log_f, 0.0)
    r_i = lax.broadcasted_iota(jnp.int32, (L, L), 0)
    c_i = lax.broadcasted_iota(jnp.int32, (L, L), 1)
    causal = c_i <= r_i
    tri = jnp.where(causal, 1.0, 0.0).astype(BF16)
    bcum = sum(_dot(tri, part) for part in _split3(log_f))
    xcol = jnp.where(lane < M_HEADS, zif, bcum)
    xrow = xcol.T
    ones = jnp.ones((L, dh), BF16)

    for h in range(M_HEADS):
        hs = slice(h * dh, (h + 1) * dh)
        li_col = xcol[:, h:h + 1]
        b_col = xcol[:, M_HEADS + h:M_HEADS + h + 1]
        li_row = xrow[h:h + 1, :]
        b_row = xrow[M_HEADS + h:M_HEADS + h + 1, :]
        m_prev = m_ref[h:h + 1, 0:1]
        b_last = b_col[L - 1:L, :]
        q = q_ref[0, :, hs]
        k = k_ref[0, :, hs]
        vaug = jnp.concatenate([v_ref[0, :, hs], ones], axis=1)
        ct = ct_ref[h]

        s = _dot_nt(q, k)
        dmat = jnp.where(causal, b_col - b_row + li_row, -jnp.inf)
        inter = b_col + m_prev
        m_t = jnp.maximum(inter, jnp.max(dmat, axis=1, keepdims=True))
        sw = (s * jnp.exp(dmat - m_t)).astype(BF16)
        nd = jnp.exp(inter - m_t) * _dot(q, ct.astype(BF16)) + _dot(sw, vaug)
        hval = nd[:, :dh] / jnp.maximum(jnp.abs(nd[:, dh:]), jnp.exp(-m_t))
        hn = _rms(hval, g_ref[:, hs])
        ym_ref[0, :, hs] = (hn * jax.nn.sigmoid(o_ref[0, :, hs].astype(F32))).astype(BF16)

        a_col = b_last - b_col + li_col
        m_new = jnp.maximum(b_last + m_prev, jnp.max(a_col, axis=0, keepdims=True))
        kw_t = (k.astype(F32) * jnp.exp(a_col - m_new)).T.astype(BF16)
        ct_ref[h] = jnp.exp(b_last + m_prev - m_new) * ct + _dot(kw_t, vaug)
        m_ref[h:h + 1, :] = jnp.broadcast_to(m_new, (1, LANES))


def _mlstm(qm, km, vm, om, zif, m_norm_g):
    bsz, seq, width = qm.shape
    L = MLSTM_CHUNK
    row = lambda b, c: (b, c, 0)
    return pl.pallas_call(
        _mlstm_kernel,
        grid=(bsz, seq // L),
        in_specs=[pl.BlockSpec((1, L, width), row)] * 4
        + [pl.BlockSpec((1, L, LANES), row), _const_spec((1, width))],
        out_specs=pl.BlockSpec((1, L, width), row),
        out_shape=jax.ShapeDtypeStruct((bsz, seq, width), BF16),
        scratch_shapes=[pltpu.VMEM((M_HEADS, M_HEAD_DIM, 2 * M_HEAD_DIM), F32),
                        pltpu.VMEM((SUBLANES, LANES), F32)],
        compiler_params=_cparams(("arbitrary", "arbitrary")),
        name="mlstm",
    )(qm, km, vm, om, zif, m_norm_g)


def _moba_kernel(qt_ref, k_ref, vt_ref, kmean_ref, o_ref, qaug_ref, s_ref, p_ref, acc_ref):
    tq = qt_ref.shape[2]
    blk = MOBA_BLOCK
    nb = k_ref.shape[1] // blk
    dh = A_HEAD_DIM
    i = pl.program_id(2)
    qt = qt_ref[0].astype(F32)
    feat = lax.broadcasted_iota(jnp.int32, (LANES, tq), 0)
    kmean = kmean_ref[0]
    blk_id = lax.broadcasted_iota(jnp.int32, (nb, tq), 0)
    valid = blk_id < i

    q_own, q_aug = [], []
    for hh in range(2):
        in_head = (feat >= hh * dh) & (feat < (hh + 1) * dh)
        qh = jnp.where(in_head, qt, 0.0).astype(BF16)
        g = jnp.where(valid, _dot(kmean, qh), -jnp.inf)
        sel = jnp.zeros(g.shape, jnp.bool_)
        for _ in range(MOBA_TOPK):
            mx = jnp.max(g, axis=0, keepdims=True)
            idx = jnp.min(jnp.where(g == mx, blk_id, nb), axis=0, keepdims=True)
            pick = (blk_id == idx) & valid
            sel = sel | pick
            g = jnp.where(pick, -jnp.inf, g)
        bias = jnp.where(sel, 0.0, NEG_BIG).astype(BF16)
        q_own.append(qh)
        q_aug.append(jnp.concatenate([qh, bias, jnp.zeros((LANES - nb, tq), BF16)], axis=0))
    qaug_ref[...] = jnp.concatenate(q_aug, axis=1)
    lane_k = lax.broadcasted_iota(jnp.int32, (blk, LANES), 1)

    def fold8(x, op):
        parts = [x[r:r + SUBLANES] for r in range(0, x.shape[0], SUBLANES)]
        while len(parts) > 1:
            parts = [op(a, b) for a, b in zip(parts[0::2], parts[1::2])]
        return parts[0]

    def put_scores(j, slot):
        st = pl.multiple_of(j * blk, blk)
        onehot = jnp.where(lane_k == j, 1.0, 0.0).astype(BF16)
        k_aug = jnp.concatenate([k_ref[0, pl.ds(st, blk), :], onehot], axis=1)
        sj = _dot(k_aug, qaug_ref[...])
        s_ref[slot] = sj
        return jnp.max(fold8(sj, jnp.maximum), axis=0, keepdims=True)

    def softmax_into(s_slot, p_slot, m_new):
        rows = 4 * SUBLANES
        for r in range(0, blk, rows):
            x = (s_ref[s_slot, r:r + rows, :] - m_new).astype(BF16)
            p_ref[p_slot, r:r + rows, :] = jnp.exp2(x)

    ones_rows = jnp.ones((ACC_ROWS - dh, blk), BF16)

    def add_pv(slot, vidx, alpha):
        st = pl.multiple_of(vidx * blk, blk)
        for hh in range(2):
            qs = slice(hh * tq, (hh + 1) * tq)
            v_aug = jnp.concatenate([vt_ref[0, hh * dh:(hh + 1) * dh, pl.ds(st, blk)], ones_rows], axis=0)
            acc_ref[hh] = alpha[:, qs] * acc_ref[hh] + _dot(v_aug, p_ref[slot, :, qs])

    start = pl.multiple_of(i * blk, blk)
    s = _dot(k_ref[0, pl.ds(start, blk), :], jnp.concatenate(q_own, axis=1))
    key_i = lax.broadcasted_iota(jnp.int32, (blk, 2 * tq), 0)
    qry_i = lax.broadcasted_iota(jnp.int32, (blk, 2 * tq), 1) & (tq - 1)
    s_ref[1] = jnp.where(key_i <= qry_i, s, NEG_BIG)
    m0 = jnp.max(fold8(s_ref[1], jnp.maximum), axis=0, keepdims=True)
    softmax_into(1, 1, m0)
    acc_ref[...] = jnp.zeros(acc_ref.shape, F32)
    cmax0 = put_scores(0, 0)

    def step(j, slot, state):
        cmax, alpha_pend, vidx, m = state
        cmax_next = put_scores(jnp.minimum(j + 1, nb - 1), 1 - slot)
        add_pv(1 - slot, vidx, alpha_pend)
        m_new = jnp.maximum(m, cmax)
        softmax_into(slot, slot, m_new)
        return cmax_next, jnp.exp2(m - m_new), j, m_new

    def body(t, state):
        return step(2 * t + 1, 1, step(2 * t, 0, state))

    state = lax.fori_loop(0, (i + 1) // 2, body, (cmax0, jnp.ones_like(m0), i, m0))
    _, alpha_pend, vidx, _ = state
    add_pv(1, vidx, alpha_pend)
    out = jnp.concatenate([acc_ref[hh, :dh, :] / acc_ref[hh, dh:dh + 1, :] for hh in range(2)],
                          axis=0)
    o_ref[0] = out.T.astype(BF16)


def _moba(qat, ka, vat, kmean):
    bsz, width, seq = qat.shape
    tq = MOBA_BLOCK
    npair = width // LANES
    return pl.pallas_call(
        _moba_kernel,
        grid=(bsz, npair, seq // tq),
        in_specs=[pl.BlockSpec((1, LANES, tq), lambda b, p, i: (b, p, i)),
                  pl.BlockSpec((1, seq, LANES), lambda b, p, i: (b, 0, p)),
                  pl.BlockSpec((1, LANES, seq), lambda b, p, i: (b, p, 0)),
                  pl.BlockSpec((1, kmean.shape[1], LANES), lambda b, p, i: (b, 0, p))],
        out_specs=pl.BlockSpec((1, tq, LANES), lambda b, p, i: (b, i, p)),
        out_shape=jax.ShapeDtypeStruct((bsz, seq, width), BF16),
        scratch_shapes=[pltpu.VMEM((2 * LANES, 2 * tq), BF16),
                        pltpu.VMEM((2, MOBA_BLOCK, 2 * tq), F32),
                        pltpu.VMEM((2, MOBA_BLOCK, 2 * tq), BF16),
                        pltpu.VMEM((2, ACC_ROWS, tq), F32)],
        compiler_params=_cparams(("arbitrary", "arbitrary", "arbitrary")),
        name="moba",
    )(qat, ka, vat, kmean)


def _merge_kernel(x_ref, ym_ref, ya_ref, sga_ref, sgb_ref, mod_ref, pm_ref, pa_ref, wo_ref, o_ref):
    merged = (sga_ref[0].astype(F32) * _dot(ym_ref[0], pm_ref[...])
              + sgb_ref[0].astype(F32) * _dot(ya_ref[0], pa_ref[...]))
    g1 = mod_ref[0, 2:3, :]
    o_ref[0] = x_ref[0] + g1 * _dot(merged.astype(BF16), wo_ref[...])


def _merge_out(x, ym, ya, sga, sgb, mod3, p_m, p_a, w_out):
    bsz, seq, d = x.shape
    tm = ROW_TILE
    row = lambda b, i: (b, i, 0)
    return pl.pallas_call(
        _merge_kernel,
        grid=(bsz, seq // tm),
        in_specs=[pl.BlockSpec((1, tm, d), row),
                  pl.BlockSpec((1, tm, ym.shape[2]), row),
                  pl.BlockSpec((1, tm, ya.shape[2]), row),
                  pl.BlockSpec((1, tm, d), row),
                  pl.BlockSpec((1, tm, d), row),
                  pl.BlockSpec((1, N_MOD, d), lambda b, i: (b, 0, 0)),
                  _const_spec(p_m.shape), _const_spec(p_a.shape), _const_spec(w_out.shape)],
        out_specs=pl.BlockSpec((1, tm, d), row),
        out_shape=jax.ShapeDtypeStruct((bsz, seq, d), F32),
        compiler_params=_cparams(("arbitrary", "arbitrary")),
        name="merge_out",
    )(x, ym, ya, sga, sgb, mod3, p_m, p_a, w_out)


def _ffn_kernel(x_ref, mod_ref, g2_ref, gf_ref, wg_ref, wu_ref, wd_ref, o_ref, *, final_norm):
    x = x_ref[0]
    sh = mod_ref[0, 3:4, :]
    sc = mod_ref[0, 4:5, :]
    gate = mod_ref[0, 5:6, :]
    h = (_rms(x, g2_ref[...]) * (1.0 + sc) + sh).astype(BF16)
    d_ff = wg_ref.shape[1]
    step = 256
    acc = jnp.zeros(x.shape, F32)
    for c0 in range(0, d_ff, step):
        a = _dot(h, wg_ref[:, c0:c0 + step])
        u = _dot(h, wu_ref[:, c0:c0 + step])
        f = (a * jax.nn.sigmoid(a) * u).astype(BF16)
        acc = acc + _dot(f, wd_ref[c0:c0 + step, :])
    y = x + gate * acc
    o_ref[0] = _rms(y, gf_ref[...]) if final_norm else y


def _ffn_out(x, mod3, norm2_g, normf_g, w_gate, w_up, w_down, final_norm):
    bsz, seq, d = x.shape
    tm = ROW_TILE
    row = lambda b, i: (b, i, 0)
    return pl.pallas_call(
        functools.partial(_ffn_kernel, final_norm=final_norm),
        grid=(bsz, seq // tm),
        in_specs=[pl.BlockSpec((1, tm, d), row),
                  pl.BlockSpec((1, N_MOD, d), lambda b, i: (b, 0, 0)),
                  _const_spec((1, d)), _const_spec((1, d)),
                  _const_spec(w_gate.shape), _const_spec(w_up.shape), _const_spec(w_down.shape)],
        out_specs=pl.BlockSpec((1, tm, d), row),
        out_shape=jax.ShapeDtypeStruct((bsz, seq, d), F32),
        compiler_params=_cparams(("arbitrary", "arbitrary")),
        name="ffn_out",
    )(x, mod3, norm2_g, normf_g, w_gate, w_up, w_down)


def _pack_in_proj(w_in, b_in):
    m_end = 4 * M_WIDTH
    if_end = m_end + 2 * M_HEADS
    pad = LANES - 2 * M_HEADS

    def pack(t):
        return jnp.concatenate(
            [t[..., :m_end], t[..., if_end:], t[..., m_end:if_end],
             jnp.zeros(t.shape[:-1] + (pad,), t.dtype)], axis=-1)

    return pack(w_in).astype(BF16), pack(b_in.reshape(1, -1))


def kernel(x, c, positions, ada_w, ada_b, norm1_g, norm2_g, normf_g, w_in, b_in, conv_w, conv_b,
           m_norm_g, p_mlstm, p_moba, w_out, w_gate, w_up, w_down):
    bsz, seq, d = x.shape
    depth = ada_w.shape[0]
    n_blocks = seq // MOBA_BLOCK
    cos_t, sin_t = _rope_tables(positions)
    for l in range(depth):
        mod3 = _ada_mod(c, ada_w[l], ada_b[l]).reshape(bsz, N_MOD, d)
        w_packed, b_packed = _pack_in_proj(w_in[l], b_in[l])
        (qm, km, vm, om, qa, ka, va, sga, sgb, zif, kmean_tiles) = _in_proj(
            x, mod3, norm1_g[l].reshape(1, d), w_packed, b_packed, conv_w[l],
            conv_b[l].reshape(1, -1), cos_t, sin_t)
        ym = _mlstm(qm, km, vm, om, zif, m_norm_g[l].reshape(1, -1))
        kmean = kmean_tiles[:, :ROW_TILE // MOBA_BLOCK].reshape(bsz, n_blocks, A_WIDTH)
        ya = _moba(qa, ka, va, kmean.astype(BF16))
        x = _merge_out(x, ym, ya, sga, sgb, mod3, p_mlstm[l].astype(BF16), p_moba[l].astype(BF16),
                       w_out[l].astype(BF16))
        x = _ffn_out(x, mod3, norm2_g[l].reshape(1, d), normf_g.reshape(1, d),
                     w_gate[l].astype(BF16), w_up[l].astype(BF16), w_down[l].astype(BF16),
                     final_norm=(l == depth - 1))
    return x
```

```python
import functools

import jax
import jax.numpy as jnp
from jax import lax
from jax.experimental import pallas as pl
from jax.experimental.pallas import tpu as pltpu

F32 = jnp.float32
BF16 = jnp.bfloat16

M_HEADS = 4
M_HEAD_DIM = 128
M_WIDTH = M_HEADS * M_HEAD_DIM
M_CONV = 4
A_HEADS = 8
A_HEAD_DIM = 64
A_WIDTH = A_HEADS * A_HEAD_DIM
MOBA_BLOCK = 256
MOBA_TOPK = 3
ROPE_THETA = 10000.0
N_MOD = 6
RMS_EPS = 1e-6

LANES = 128
SUBLANES = 8
VMEM_LIMIT = 56 * 1024 * 1024

ROW_TILE = 512
MLSTM_CHUNK = 256
NEG_BIG = -1e30
LOG2_E = 1.4426950408889634
MOBA_UNROLL = 4
ACC_ROWS = A_HEAD_DIM + 16

OFF_MQ, OFF_MK, OFF_MV, OFF_MO = 0, 512, 1024, 1536
OFF_AQ, OFF_AK, OFF_AV = 2048, 2560, 3072
OFF_GA, OFF_GB = 3584, 4608
OFF_IF = 5632
D_IN_PACKED = 5760


def _cparams(sem, flags=None):
    return pltpu.CompilerParams(dimension_semantics=sem, vmem_limit_bytes=VMEM_LIMIT, flags=flags)


def _const_spec(shape):
    nd = len(shape)
    return pl.BlockSpec(shape, lambda *_: (0,) * nd, pipeline_mode=pl.Buffered(1))


def _rms(x, g):
    ms = jnp.mean(x * x, axis=-1, keepdims=True)
    return x * lax.rsqrt(ms + RMS_EPS) * g


def _dot(a, b):
    return jnp.dot(a, b, preferred_element_type=F32)


def _dot_nt(a, b):
    return lax.dot_general(a, b, (((1,), (1,)), ((), ())), preferred_element_type=F32)


def _ada_kernel(c_ref, w_ref, b_ref, o_ref):
    c = c_ref[...]
    ca = c * jax.nn.sigmoid(c)
    o_ref[...] = jnp.dot(ca, w_ref[...], precision=lax.Precision.HIGHEST,
                         preferred_element_type=F32) + b_ref[...]


def _ada_mod(c, ada_w, ada_b):
    bsz, d = c.shape
    n = ada_w.shape[1]
    tn = d
    return pl.pallas_call(
        _ada_kernel,
        grid=(n // tn,),
        in_specs=[pl.BlockSpec((bsz, d), lambda j: (0, 0)),
                  pl.BlockSpec((d, tn), lambda j: (0, j)),
                  pl.BlockSpec((1, tn), lambda j: (0, j))],
        out_specs=pl.BlockSpec((bsz, tn), lambda j: (0, j)),
        out_shape=jax.ShapeDtypeStruct((bsz, n), F32),
        compiler_params=_cparams(("arbitrary",)),
        name="ada_mod",
    )(c, ada_w, ada_b.reshape(1, n))


def _rope_kernel(pos_ref, freq_ref, cos_ref, sin_ref):
    pos = pos_ref[0].astype(F32)
    ang = pos * freq_ref[...]
    lane = lax.broadcasted_iota(jnp.int32, ang.shape, 1)
    first_half = (lane % A_HEAD_DIM) < (A_HEAD_DIM // 2)
    s = jnp.sin(ang)
    cos_ref[0] = jnp.cos(ang)
    sin_ref[0] = jnp.where(first_half, -s, s)


def _rope_tables(positions):
    bsz, seq = positions.shape
    half = A_HEAD_DIM // 2
    inv_freq = ROPE_THETA ** (-jnp.arange(half, dtype=F32) / half)
    freq = jnp.tile(inv_freq, LANES // half).reshape(1, LANES)
    ts = 2048
    out = jax.ShapeDtypeStruct((bsz, seq, LANES), F32)
    return pl.pallas_call(
        _rope_kernel,
        grid=(bsz, seq // ts),
        in_specs=[pl.BlockSpec((1, ts, 1), lambda b, i: (b, i, 0)),
                  pl.BlockSpec((1, LANES), lambda b, i: (0, 0))],
        out_specs=[pl.BlockSpec((1, ts, LANES), lambda b, i: (b, i, 0))] * 2,
        out_shape=[out, out],
        compiler_params=_cparams(("arbitrary", "arbitrary")),
        name="rope_table",
    )(positions.reshape(bsz, seq, 1), freq)


def _in_proj_kernel(x_ref, mod_ref, g_ref, w_ref, b_ref, cw_ref, cb_ref, cos_ref, sin_ref,
                    qm_ref, km_ref, vm_ref, om_ref, qa_ref, ka_ref, va_ref,
                    sga_ref, sgb_ref, zif_ref, kmean_ref, hb_ref, zbuf_ref):
    tm = x_ref.shape[1]
    it = pl.program_id(1)

    x = x_ref[0]
    sh = mod_ref[0, 0:1, :]
    sc = mod_ref[0, 1:2, :]
    hb_ref[...] = (_rms(x, g_ref[...]) * (1.0 + sc) + sh).astype(BF16)

    def proj(off, width):
        return _dot(hb_ref[...], w_ref[:, off:off + width]) + b_ref[:, off:off + width]

    @pl.when(it == 0)
    def _():
        zbuf_ref[0:SUBLANES, :] = jnp.zeros((SUBLANES, zbuf_ref.shape[1]), F32)

    @pl.when(it > 0)
    def _():
        zbuf_ref[0:SUBLANES, :] = zbuf_ref[tm:tm + SUBLANES, :]

    for half, (off, out_ref, scale) in enumerate(((OFF_MQ, qm_ref, 1.0),
                                                  (OFF_MK, km_ref, M_HEAD_DIM ** -0.5))):
        cols = slice(half * M_WIDTH, (half + 1) * M_WIDTH)
        zbuf_ref[SUBLANES:SUBLANES + tm, cols] = proj(off, M_WIDTH)
        acc = jnp.broadcast_to(cb_ref[:, cols], (tm, M_WIDTH))
        for j in range(M_CONV):
            start = SUBLANES - (M_CONV - 1) + j
            acc = acc + zbuf_ref[start:start + tm, cols] * cw_ref[j:j + 1, cols]
        y = acc * jax.nn.sigmoid(acc)
        out_ref[0] = (y * scale).astype(BF16)

    vm_ref[0] = proj(OFF_MV, M_WIDTH).astype(BF16)
    om_ref[0] = proj(OFF_MO, M_WIDTH).astype(BF16)

    def rope(z):
        outs = []
        for p in range(A_WIDTH // LANES):
            zp = z[:, p * LANES:(p + 1) * LANES]
            lane = lax.broadcasted_iota(jnp.int32, zp.shape, 1)
            first_half = (lane % A_HEAD_DIM) < (A_HEAD_DIM // 2)
            rot = jnp.where(first_half,
                            pltpu.roll(zp, LANES - A_HEAD_DIM // 2, 1),
                            pltpu.roll(zp, A_HEAD_DIM // 2, 1))
            outs.append(zp * cos_ref[0] + rot * sin_ref[0])
        return outs

    q_parts = rope(proj(OFF_AQ, A_WIDTH))
    for p, qp in enumerate(q_parts):
        qa_ref[0, p * LANES:(p + 1) * LANES, :] = (qp * (LOG2_E * A_HEAD_DIM ** -0.5)).T.astype(BF16)

    k_parts = rope(proj(OFF_AK, A_WIDTH))
    kmean_ref[0] = jnp.zeros(kmean_ref.shape[1:], F32)
    for p, kp in enumerate(k_parts):
        ka_ref[0, :, p * LANES:(p + 1) * LANES] = kp.astype(BF16)
        for blk in range(tm // MOBA_BLOCK):
            kmean_ref[0, blk:blk + 1, p * LANES:(p + 1) * LANES] = jnp.mean(
                kp[blk * MOBA_BLOCK:(blk + 1) * MOBA_BLOCK], axis=0, keepdims=True)

    va = proj(OFF_AV, A_WIDTH)
    for p in range(A_WIDTH // LANES):
        va_ref[0, p * LANES:(p + 1) * LANES, :] = va[:, p * LANES:(p + 1) * LANES].T.astype(BF16)

    d = x.shape[1]
    for off, out_ref in ((OFF_GA, sga_ref), (OFF_GB, sgb_ref)):
        for c0 in range(0, d, 512):
            out_ref[0, :, c0:c0 + 512] = jax.nn.sigmoid(proj(off + c0, 512)).astype(BF16)

    zif_ref[0] = proj(OFF_IF, LANES)


def _in_proj(x, mod3, norm_g, w_packed, b_packed, conv_w, conv_b, cos_t, sin_t):
    bsz, seq, d = x.shape
    tm = ROW_TILE
    nt = seq // tm
    row = lambda b, i: (b, i, 0)

    def rows(width, dtype):
        return (pl.BlockSpec((1, tm, width), row), jax.ShapeDtypeStruct((bsz, seq, width), dtype))

    cols = (pl.BlockSpec((1, A_WIDTH, tm), lambda b, i: (b, 0, i)),
            jax.ShapeDtypeStruct((bsz, A_WIDTH, seq), BF16))
    outs = ([rows(M_WIDTH, BF16)] * 4 + [cols, rows(A_WIDTH, BF16), cols]
            + [rows(d, BF16)] * 2 + [rows(LANES, F32)])
    outs.append((pl.BlockSpec((1, SUBLANES, A_WIDTH), lambda b, i: (b * nt + i, 0, 0)),
                 jax.ShapeDtypeStruct((bsz * nt, SUBLANES, A_WIDTH), F32)))
    return pl.pallas_call(
        _in_proj_kernel,
        grid=(bsz, nt),
        in_specs=[pl.BlockSpec((1, tm, d), row),
                  pl.BlockSpec((1, N_MOD, d), lambda b, i: (b, 0, 0)),
                  _const_spec((1, d)),
                  _const_spec(w_packed.shape),
                  _const_spec(b_packed.shape),
                  _const_spec(conv_w.shape),
                  _const_spec(conv_b.shape),
                  pl.BlockSpec((1, tm, LANES), row),
                  pl.BlockSpec((1, tm, LANES), row)],
        out_specs=[o[0] for o in outs],
        out_shape=[o[1] for o in outs],
        scratch_shapes=[pltpu.VMEM((tm, d), BF16),
                        pltpu.VMEM((tm + SUBLANES, 2 * M_WIDTH), F32)],
        compiler_params=_cparams(("arbitrary", "arbitrary")),
        name="in_proj",
    )(x, mod3, norm_g, w_packed, b_packed, conv_w, conv_b, cos_t, sin_t)


def _split3(x):
    hi = x.astype(BF16)
    r1 = x - hi.astype(F32)
    mid = r1.astype(BF16)
    lo = (r1 - mid.astype(F32)).astype(BF16)
    return hi, mid, lo


def _mlstm_kernel(q_ref, k_ref, v_ref, o_ref, zif_ref, g_ref, ym_ref, ct_ref, m_ref):
    bsz, L = q_ref.shape[0], q_ref.shape[1]
    dh = M_HEAD_DIM

    @pl.when(pl.program_id(0) == 0)
    def _():
        ct_ref[...] = jnp.zeros(ct_ref.shape, F32)
        m_ref[...] = jnp.zeros(m_ref.shape, F32)

    r_i = lax.broadcasted_iota(jnp.int32, (L, L), 0)
    c_i = lax.broadcasted_iota(jnp.int32, (L, L), 1)
    causal = c_i <= r_i
    tri = jnp.where(causal, 1.0, 0.0).astype(BF16)
    ones = jnp.ones((L, dh), BF16)

    for b in range(bsz):
        zif = zif_ref[b]
        lane = lax.broadcasted_iota(jnp.int32, zif.shape, 1)
        log_f = jnp.minimum(zif, 0.0) - jnp.log1p(jnp.exp(-jnp.abs(zif)))
        log_f = jnp.where((lane >= M_HEADS) & (lane < 2 * M_HEADS), log_f, 0.0)
        bcum = sum(_dot(tri, part) for part in _split3(log_f))
        xcol = jnp.where(lane < M_HEADS, zif, bcum)
        xrow = xcol.T

        for h in range(M_HEADS):
            hs = slice(h * dh, (h + 1) * dh)
            st = b * M_HEADS + h
            li_col = xcol[:, h:h + 1]
            b_col = xcol[:, M_HEADS + h:M_HEADS + h + 1]
            li_row = xrow[h:h + 1, :]
            b_row = xrow[M_HEADS + h:M_HEADS + h + 1, :]
            m_prev = m_ref[st:st + 1, 0:1]
            b_last = b_col[L - 1:L, :]
            q = q_ref[b, :, hs]
            k = k_ref[b, :, hs]
            vaug = jnp.concatenate([v_ref[b, :, hs], ones], axis=1)
            ct = ct_ref[st]

            s = _dot_nt(q, k)
            dmat = jnp.where(causal, b_col + (li_row - b_row), -jnp.inf)
            inter = b_col + m_prev
            m_t = jnp.maximum(inter, jnp.max(dmat, axis=1, keepdims=True))
            sw = (s * jnp.exp(dmat - m_t)).astype(BF16)
            nd = jnp.exp(inter - m_t) * _dot(q, ct.astype(BF16)) + _dot(sw, vaug)
            hval = nd[:, :dh] / jnp.maximum(jnp.abs(nd[:, dh:]), jnp.exp(-m_t))
            hn = _rms(hval, g_ref[:, hs])
            ym_ref[b, :, hs] = (hn * jax.nn.sigmoid(o_ref[b, :, hs].astype(F32))).astype(BF16)

            a_col = b_last - b_col + li_col
            m_new = jnp.maximum(b_last + m_prev, jnp.max(a_col, axis=0, keepdims=True))
            kw_t = (k.astype(F32) * jnp.exp(a_col - m_new)).T.astype(BF16)
            ct_ref[st] = jnp.exp(b_last + m_prev - m_new) * ct + _dot(kw_t, vaug)
            m_ref[st:st + 1, :] = jnp.broadcast_to(m_new, (1, LANES))


def _mlstm(qm, km, vm, om, zif, m_norm_g):
    bsz, seq, width = qm.shape
    L = MLSTM_CHUNK
    row = lambda c: (0, c, 0)
    return pl.pallas_call(
        _mlstm_kernel,
        grid=(seq // L,),
        in_specs=[pl.BlockSpec((bsz, L, width), row)] * 4
        + [pl.BlockSpec((bsz, L, LANES), row), _const_spec((1, width))],
        out_specs=pl.BlockSpec((bsz, L, width), row),
        out_shape=jax.ShapeDtypeStruct((bsz, seq, width), BF16),
        scratch_shapes=[pltpu.VMEM((bsz * M_HEADS, M_HEAD_DIM, 2 * M_HEAD_DIM), F32),
                        pltpu.VMEM((bsz * M_HEADS, LANES), F32)],
        compiler_params=_cparams(("arbitrary",)),
        name="mlstm",
    )(qm, km, vm, om, zif, m_norm_g)


def _moba_kernel(qt_ref, k_ref, vt_ref, kmean_ref, o_ref, qaug_ref, s_ref, p_ref, acc_ref):
    tq = qt_ref.shape[2]
    blk = MOBA_BLOCK
    nb = k_ref.shape[1] // blk
    dh = A_HEAD_DIM
    i = pl.program_id(2)
    qt = qt_ref[0].astype(F32)
    feat = lax.broadcasted_iota(jnp.int32, (LANES, tq), 0)
    kmean = kmean_ref[0]
    blk_id = lax.broadcasted_iota(jnp.int32, (nb, tq), 0)
    valid = blk_id < i

    q_own, q_aug = [], []
    for hh in range(2):
        in_head = (feat >= hh * dh) & (feat < (hh + 1) * dh)
        qh = jnp.where(in_head, qt, 0.0).astype(BF16)
        g = jnp.where(valid, _dot(kmean, qh), -jnp.inf)
        sel = jnp.zeros(g.shape, jnp.bool_)
        for _ in range(MOBA_TOPK):
            mx = jnp.max(g, axis=0, keepdims=True)
            idx = jnp.min(jnp.where(g == mx, blk_id, nb), axis=0, keepdims=True)
            pick = (blk_id == idx) & valid
            sel = sel | pick
            g = jnp.where(pick, -jnp.inf, g)
        bias = jnp.where(sel, 0.0, NEG_BIG).astype(BF16)
        q_own.append(qh)
        q_aug.append(jnp.concatenate([qh, bias, jnp.zeros((LANES - nb, tq), BF16)], axis=0))
    qaug_ref[...] = jnp.concatenate(q_aug, axis=1)
    lane_k = lax.broadcasted_iota(jnp.int32, (blk, LANES), 1)

    def fold8(x, op):
        parts = [x[r:r + SUBLANES] for r in range(0, x.shape[0], SUBLANES)]
        while len(parts) > 1:
            parts = [op(a, b) for a, b in zip(parts[0::2], parts[1::2])]
        return parts[0]

    def put_scores(j, slot):
        st = pl.multiple_of(j * blk, blk)
        onehot = jnp.where(lane_k == j, 1.0, 0.0).astype(BF16)
        k_aug = jnp.concatenate([k_ref[0, pl.ds(st, blk), :], onehot], axis=1)
        sj = _dot(k_aug, qaug_ref[...])
        s_ref[slot] = sj
        return jnp.max(fold8(sj, jnp.maximum), axis=0, keepdims=True)

    def softmax_into(s_slot, p_slot, m_new):
        rows = 4 * SUBLANES
        for r in range(0, blk, rows):
            x = (s_ref[s_slot, r:r + rows, :] - m_new).astype(BF16)
            p_ref[p_slot, r:r + rows, :] = jnp.exp2(x)

    ones_rows = jnp.ones((ACC_ROWS - dh, blk), BF16)

    def add_pv(slot, vidx, alpha):
        st = pl.multiple_of(vidx * blk, blk)
        for hh in range(2):
            qs = slice(hh * tq, (hh + 1) * tq)
            v_aug = jnp.concatenate([vt_ref[0, hh * dh:(hh + 1) * dh, pl.ds(st, blk)], ones_rows], axis=0)
            acc_ref[hh] = alpha[:, qs] * acc_ref[hh] + _dot(v_aug, p_ref[slot, :, qs])

    start = pl.multiple_of(i * blk, blk)
    s = _dot(k_ref[0, pl.ds(start, blk), :], jnp.concatenate(q_own, axis=1))
    key_i = lax.broadcasted_iota(jnp.int32, (blk, 2 * tq), 0)
    qry_i = lax.broadcasted_iota(jnp.int32, (blk, 2 * tq), 1) & (tq - 1)
    s_ref[1] = jnp.where(key_i <= qry_i, s, NEG_BIG)
    m0 = jnp.max(fold8(s_ref[1], jnp.maximum), axis=0, keepdims=True)
    softmax_into(1, 1, m0)
    acc_ref[...] = jnp.zeros(acc_ref.shape, F32)
    cmax0 = put_scores(0, 0)

    def step(j, slot, state):
        cmax, alpha_pend, vidx, m = state
        cmax_next = put_scores(jnp.minimum(j + 1, nb - 1), 1 - slot)
        add_pv(1 - slot, vidx, alpha_pend)
        m_new = jnp.maximum(m, cmax)
        softmax_into(slot, slot, m_new)
        return cmax_next, jnp.exp2(m - m_new), j, m_new

    def run(first_block, n_trips, unroll, state):
        def body(t, st):
            for u in range(unroll):
                st = step(first_block + unroll * t + u, u % 2, st)
            return st
        return lax.fori_loop(0, n_trips, body, state)

    long_trips = i // MOBA_UNROLL
    done = long_trips * MOBA_UNROLL
    state = run(0, long_trips, MOBA_UNROLL, (cmax0, jnp.ones_like(m0), i, m0))
    state = run(done, (i - done + 1) // 2, 2, state)
    _, alpha_pend, vidx, _ = state
    add_pv(1, vidx, alpha_pend)
    out = jnp.concatenate([acc_ref[hh, :dh, :] / acc_ref[hh, dh:dh + 1, :] for hh in range(2)],
                          axis=0)
    o_ref[0] = out.T.astype(BF16)


def _moba(qat, ka, vat, kmean):
    bsz, width, seq = qat.shape
    tq = MOBA_BLOCK
    npair = width // LANES
    return pl.pallas_call(
        _moba_kernel,
        grid=(bsz, npair, seq // tq),
        in_specs=[pl.BlockSpec((1, LANES, tq), lambda b, p, i: (b, p, i)),
                  pl.BlockSpec((1, seq, LANES), lambda b, p, i: (b, 0, p)),
                  pl.BlockSpec((1, LANES, seq), lambda b, p, i: (b, p, 0)),
                  pl.BlockSpec((1, kmean.shape[1], LANES), lambda b, p, i: (b, 0, p))],
        out_specs=pl.BlockSpec((1, tq, LANES), lambda b, p, i: (b, i, p)),
        out_shape=jax.ShapeDtypeStruct((bsz, seq, width), BF16),
        scratch_shapes=[pltpu.VMEM((2 * LANES, 2 * tq), BF16),
                        pltpu.VMEM((2, MOBA_BLOCK, 2 * tq), F32),
                        pltpu.VMEM((2, MOBA_BLOCK, 2 * tq), BF16),
                        pltpu.VMEM((2, ACC_ROWS, tq), F32)],
        compiler_params=_cparams(("arbitrary", "arbitrary", "arbitrary")),
        name="moba",
    )(qat, ka, vat, kmean)


def _merge_kernel(x_ref, ym_ref, ya_ref, sga_ref, sgb_ref, mod_ref, pm_ref, pa_ref, wo_ref, o_ref):
    merged = (sga_ref[0].astype(F32) * _dot(ym_ref[0], pm_ref[...])
              + sgb_ref[0].astype(F32) * _dot(ya_ref[0], pa_ref[...]))
    g1 = mod_ref[0, 2:3, :]
    o_ref[0] = x_ref[0] + g1 * _dot(merged.astype(BF16), wo_ref[...])


def _merge_out(x, ym, ya, sga, sgb, mod3, p_m, p_a, w_out):
    bsz, seq, d = x.shape
    tm = ROW_TILE
    row = lambda b, i: (b, i, 0)
    return pl.pallas_call(
        _merge_kernel,
        grid=(bsz, seq // tm),
        in_specs=[pl.BlockSpec((1, tm, d), row),
                  pl.BlockSpec((1, tm, ym.shape[2]), row),
                  pl.BlockSpec((1, tm, ya.shape[2]), row),
                  pl.BlockSpec((1, tm, d), row),
                  pl.BlockSpec((1, tm, d), row),
                  pl.BlockSpec((1, N_MOD, d), lambda b, i: (b, 0, 0)),
                  _const_spec(p_m.shape), _const_spec(p_a.shape), _const_spec(w_out.shape)],
        out_specs=pl.BlockSpec((1, tm, d), row),
        out_shape=jax.ShapeDtypeStruct((bsz, seq, d), F32),
        compiler_params=_cparams(("arbitrary", "arbitrary")),
        name="merge_out",
    )(x, ym, ya, sga, sgb, mod3, p_m, p_a, w_out)


def _ffn_kernel(x_ref, mod_ref, g2_ref, gf_ref, wg_ref, wu_ref, wd_ref, o_ref, *, final_norm):
    x = x_ref[0]
    sh = mod_ref[0, 3:4, :]
    sc = mod_ref[0, 4:5, :]
    gate = mod_ref[0, 5:6, :]
    h = (_rms(x, g2_ref[...]) * (1.0 + sc) + sh).astype(BF16)
    d_ff = wg_ref.shape[1]
    step = 256
    acc = jnp.zeros(x.shape, F32)
    for c0 in range(0, d_ff, step):
        a = _dot(h, wg_ref[:, c0:c0 + step])
        u = _dot(h, wu_ref[:, c0:c0 + step])
        f = (a * jax.nn.sigmoid(a) * u).astype(BF16)
        acc = acc + _dot(f, wd_ref[c0:c0 + step, :])
    y = x + gate * acc
    o_ref[0] = _rms(y, gf_ref[...]) if final_norm else y


def _ffn_out(x, mod3, norm2_g, normf_g, w_gate, w_up, w_down, final_norm):
    bsz, seq, d = x.shape
    tm = ROW_TILE
    row = lambda b, i: (b, i, 0)
    return pl.pallas_call(
        functools.partial(_ffn_kernel, final_norm=final_norm),
        grid=(bsz, seq // tm),
        in_specs=[pl.BlockSpec((1, tm, d), row),
                  pl.BlockSpec((1, N_MOD, d), lambda b, i: (b, 0, 0)),
                  _const_spec((1, d)), _const_spec((1, d)),
                  _const_spec(w_gate.shape), _const_spec(w_up.shape), _const_spec(w_down.shape)],
        out_specs=pl.BlockSpec((1, tm, d), row),
        out_shape=jax.ShapeDtypeStruct((bsz, seq, d), F32),
        compiler_params=_cparams(("arbitrary", "arbitrary")),
        name="ffn_out",
    )(x, mod3, norm2_g, normf_g, w_gate, w_up, w_down)


def _pack_in_proj(w_in, b_in):
    m_end = 4 * M_WIDTH
    if_end = m_end + 2 * M_HEADS
    pad = LANES - 2 * M_HEADS

    def pack(t, dtype):
        return jnp.concatenate(
            [t[..., :m_end].astype(dtype), t[..., if_end:].astype(dtype), t[..., m_end:if_end].astype(dtype),
             jnp.zeros(t.shape[:-1] + (pad,), dtype)], axis=-1)

    return pack(w_in, BF16), pack(b_in.reshape(1, -1), F32)


def kernel(x, c, positions, ada_w, ada_b, norm1_g, norm2_g, normf_g, w_in, b_in, conv_w, conv_b,
           m_norm_g, p_mlstm, p_moba, w_out, w_gate, w_up, w_down):
    bsz, seq, d = x.shape
    depth = ada_w.shape[0]
    n_blocks = seq // MOBA_BLOCK
    cos_t, sin_t = _rope_tables(positions)
    for l in range(depth):
        mod3 = _ada_mod(c, ada_w[l], ada_b[l]).reshape(bsz, N_MOD, d)
        w_packed, b_packed = _pack_in_proj(w_in[l], b_in[l])
        (qm, km, vm, om, qa, ka, va, sga, sgb, zif, kmean_tiles) = _in_proj(
            x, mod3, norm1_g[l].reshape(1, d), w_packed, b_packed, conv_w[l],
            conv_b[l].reshape(1, -1), cos_t, sin_t)
        ym = _mlstm(qm, km, vm, om, zif, m_norm_g[l].reshape(1, -1))
        kmean = kmean_tiles[:, :ROW_TILE // MOBA_BLOCK].reshape(bsz, n_blocks, A_WIDTH)
        ya = _moba(qa, ka, va, kmean.astype(BF16))
        x = _merge_out(x, ym, ya, sga, sgb, mod3, p_mlstm[l].astype(BF16), p_moba[l].astype(BF16),
                       w_out[l].astype(BF16))
        x = _ffn_out(x, mod3, norm2_g[l].reshape(1, d), normf_g.reshape(1, d),
                     w_gate[l].astype(BF16), w_up[l].astype(BF16), w_down[l].astype(BF16),
                     final_norm=(l == depth - 1))
    return x
```

```python
import functools

import jax
import jax.numpy as jnp
from jax import lax
from jax.experimental import pallas as pl
from jax.experimental.pallas import tpu as pltpu

F32 = jnp.float32
BF16 = jnp.bfloat16

M_HEADS = 4
M_HEAD_DIM = 128
M_WIDTH = M_HEADS * M_HEAD_DIM
M_CONV = 4
A_HEADS = 8
A_HEAD_DIM = 64
A_WIDTH = A_HEADS * A_HEAD_DIM
MOBA_BLOCK = 256
MOBA_TOPK = 3
ROPE_THETA = 10000.0
N_MOD = 6
RMS_EPS = 1e-6

LANES = 128
SUBLANES = 8
VMEM_LIMIT = 56 * 1024 * 1024

ROW_TILE = 512
MLSTM_CHUNK = 256
NEG_BIG = -1e30
LOG2_E = 1.4426950408889634
MOBA_UNROLL = 4
ACC_ROWS = A_HEAD_DIM + 16

OFF_MQ, OFF_MK, OFF_MV, OFF_MO = 0, 512, 1024, 1536
OFF_AQ, OFF_AK, OFF_AV = 2048, 2560, 3072
OFF_GA, OFF_GB = 3584, 4608
OFF_IF = 5632
D_IN_PACKED = 5760


def _cparams(sem, flags=None):
    return pltpu.CompilerParams(dimension_semantics=sem, vmem_limit_bytes=VMEM_LIMIT, flags=flags)


def _const_spec(shape):
    nd = len(shape)
    return pl.BlockSpec(shape, lambda *_: (0,) * nd, pipeline_mode=pl.Buffered(1))


def _rms(x, g):
    ms = jnp.mean(x * x, axis=-1, keepdims=True)
    return x * lax.rsqrt(ms + RMS_EPS) * g


def _dot(a, b):
    return jnp.dot(a, b, preferred_element_type=F32)


def _dot_nt(a, b):
    return lax.dot_general(a, b, (((1,), (1,)), ((), ())), preferred_element_type=F32)


def _ada_kernel(c_ref, w_ref, b_ref, o_ref):
    c = c_ref[...]
    ca = c * jax.nn.sigmoid(c)
    o_ref[...] = jnp.dot(ca, w_ref[...], precision=lax.Precision.HIGHEST,
                         preferred_element_type=F32) + b_ref[...]


def _ada_mod(c, ada_w, ada_b, layer):
    bsz, d = c.shape
    depth, _, n = ada_w.shape
    tn = d
    return pl.pallas_call(
        _ada_kernel,
        grid=(n // tn,),
        in_specs=[pl.BlockSpec((bsz, d), lambda j: (0, 0)),
                  pl.BlockSpec((None, d, tn), lambda j: (layer, 0, j)),
                  pl.BlockSpec((None, 1, tn), lambda j: (layer, 0, j))],
        out_specs=pl.BlockSpec((bsz, tn), lambda j: (0, j)),
        out_shape=jax.ShapeDtypeStruct((bsz, n), F32),
        compiler_params=_cparams(("arbitrary",)),
        name="ada_mod",
    )(c, ada_w, ada_b.reshape(depth, 1, n))


def _rope_kernel(pos_ref, freq_ref, cos_ref, sin_ref):
    pos = pos_ref[0].astype(F32)
    ang = pos * freq_ref[...]
    lane = lax.broadcasted_iota(jnp.int32, ang.shape, 1)
    first_half = (lane % A_HEAD_DIM) < (A_HEAD_DIM // 2)
    s = jnp.sin(ang)
    cos_ref[0] = jnp.cos(ang)
    sin_ref[0] = jnp.where(first_half, -s, s)


def _rope_tables(positions):
    bsz, seq = positions.shape
    half = A_HEAD_DIM // 2
    inv_freq = ROPE_THETA ** (-jnp.arange(half, dtype=F32) / half)
    freq = jnp.tile(inv_freq, LANES // half).reshape(1, LANES)
    ts = 2048
    out = jax.ShapeDtypeStruct((bsz, seq, LANES), F32)
    return pl.pallas_call(
        _rope_kernel,
        grid=(bsz, seq // ts),
        in_specs=[pl.BlockSpec((1, ts, 1), lambda b, i: (b, i, 0)),
                  pl.BlockSpec((1, LANES), lambda b, i: (0, 0))],
        out_specs=[pl.BlockSpec((1, ts, LANES), lambda b, i: (b, i, 0))] * 2,
        out_shape=[out, out],
        compiler_params=_cparams(("arbitrary", "arbitrary")),
        name="rope_table",
    )(positions.reshape(bsz, seq, 1), freq)


def _in_proj_kernel(x_ref, mod_ref, g_ref, wm_ref, wa_ref, wif_ref, bm_ref, ba_ref, bif_ref,
                    cw_ref, cb_ref, cos_ref, sin_ref,
                    qm_ref, km_ref, vm_ref, om_ref, qa_ref, ka_ref, va_ref,
                    sga_ref, sgb_ref, zif_ref, kmean_ref, hb_ref, zbuf_ref):
    tm = x_ref.shape[1]
    it = pl.program_id(1)

    x = x_ref[0]
    sh = mod_ref[0, 0:1, :]
    sc = mod_ref[0, 1:2, :]
    hb_ref[...] = (_rms(x, g_ref[...]) * (1.0 + sc) + sh).astype(BF16)

    def proj(off, width):
        for base, w_ref, b_ref in ((OFF_IF, wif_ref, bif_ref), (OFF_AQ, wa_ref, ba_ref), (OFF_MQ, wm_ref, bm_ref)):
            if off >= base:
                lo = off - base
                return _dot(hb_ref[...], w_ref[:, lo:lo + width]) + b_ref[:, lo:lo + width]

    @pl.when(it == 0)
    def _():
        zbuf_ref[0:SUBLANES, :] = jnp.zeros((SUBLANES, zbuf_ref.shape[1]), F32)

    @pl.when(it > 0)
    def _():
        zbuf_ref[0:SUBLANES, :] = zbuf_ref[tm:tm + SUBLANES, :]

    for half, (off, out_ref, scale) in enumerate(((OFF_MQ, qm_ref, 1.0),
                                                  (OFF_MK, km_ref, M_HEAD_DIM ** -0.5))):
        cols = slice(half * M_WIDTH, (half + 1) * M_WIDTH)
        zbuf_ref[SUBLANES:SUBLANES + tm, cols] = proj(off, M_WIDTH)
        acc = jnp.broadcast_to(cb_ref[:, cols], (tm, M_WIDTH))
        for j in range(M_CONV):
            start = SUBLANES - (M_CONV - 1) + j
            acc = acc + zbuf_ref[start:start + tm, cols] * cw_ref[j:j + 1, cols]
        y = acc * jax.nn.sigmoid(acc)
        out_ref[0] = (y * scale).astype(BF16)

    vm_ref[0] = proj(OFF_MV, M_WIDTH).astype(BF16)
    om_ref[0] = proj(OFF_MO, M_WIDTH).astype(BF16)

    def rope(z):
        outs = []
        for p in range(A_WIDTH // LANES):
            zp = z[:, p * LANES:(p + 1) * LANES]
            lane = lax.broadcasted_iota(jnp.int32, zp.shape, 1)
            first_half = (lane % A_HEAD_DIM) < (A_HEAD_DIM // 2)
            rot = jnp.where(first_half,
                            pltpu.roll(zp, LANES - A_HEAD_DIM // 2, 1),
                            pltpu.roll(zp, A_HEAD_DIM // 2, 1))
            outs.append(zp * cos_ref[0] + rot * sin_ref[0])
        return outs

    q_parts = rope(proj(OFF_AQ, A_WIDTH))
    for p, qp in enumerate(q_parts):
        qa_ref[0, p * LANES:(p + 1) * LANES, :] = (qp * (LOG2_E * A_HEAD_DIM ** -0.5)).T.astype(BF16)

    k_parts = rope(proj(OFF_AK, A_WIDTH))
    kmean_ref[0] = jnp.zeros(kmean_ref.shape[1:], F32)
    for p, kp in enumerate(k_parts):
        ka_ref[0, :, p * LANES:(p + 1) * LANES] = kp.astype(BF16)
        for blk in range(tm // MOBA_BLOCK):
            kmean_ref[0, blk:blk + 1, p * LANES:(p + 1) * LANES] = jnp.mean(
                kp[blk * MOBA_BLOCK:(blk + 1) * MOBA_BLOCK], axis=0, keepdims=True)

    va = proj(OFF_AV, A_WIDTH)
    for p in range(A_WIDTH // LANES):
        va_ref[0, p * LANES:(p + 1) * LANES, :] = va[:, p * LANES:(p + 1) * LANES].T.astype(BF16)

    d = x.shape[1]
    for off, out_ref in ((OFF_GA, sga_ref), (OFF_GB, sgb_ref)):
        for c0 in range(0, d, 512):
            out_ref[0, :, c0:c0 + 512] = jax.nn.sigmoid(proj(off + c0, 512)).astype(BF16)

    zif_ref[0] = proj(OFF_IF, LANES)


def _in_proj(x, mod3, norm_g, weights, biases, conv_w, conv_b, cos_t, sin_t):
    bsz, seq, d = x.shape
    tm = ROW_TILE
    nt = seq // tm
    row = lambda b, i: (b, i, 0)

    def rows(width, dtype):
        return (pl.BlockSpec((1, tm, width), row), jax.ShapeDtypeStruct((bsz, seq, width), dtype))

    cols = (pl.BlockSpec((1, A_WIDTH, tm), lambda b, i: (b, 0, i)),
            jax.ShapeDtypeStruct((bsz, A_WIDTH, seq), BF16))
    outs = ([rows(M_WIDTH, BF16)] * 4 + [cols, rows(A_WIDTH, BF16), cols]
            + [rows(d, BF16)] * 2 + [rows(LANES, F32)])
    outs.append((pl.BlockSpec((1, SUBLANES, A_WIDTH), lambda b, i: (b * nt + i, 0, 0)),
                 jax.ShapeDtypeStruct((bsz * nt, SUBLANES, A_WIDTH), F32)))
    return pl.pallas_call(
        _in_proj_kernel,
        grid=(bsz, nt),
        in_specs=[pl.BlockSpec((1, tm, d), row),
                  pl.BlockSpec((1, N_MOD, d), lambda b, i: (b, 0, 0)),
                  _const_spec((1, d)),
                  *[_const_spec(t.shape) for t in (*weights, *biases)],
                  _const_spec(conv_w.shape),
                  _const_spec(conv_b.shape),
                  pl.BlockSpec((1, tm, LANES), row),
                  pl.BlockSpec((1, tm, LANES), row)],
        out_specs=[o[0] for o in outs],
        out_shape=[o[1] for o in outs],
        scratch_shapes=[pltpu.VMEM((tm, d), BF16),
                        pltpu.VMEM((tm + SUBLANES, 2 * M_WIDTH), F32)],
        compiler_params=_cparams(("arbitrary", "arbitrary")),
        name="in_proj",
    )(x, mod3, norm_g, *weights, *biases, conv_w, conv_b, cos_t, sin_t)


def _split3(x):
    hi = x.astype(BF16)
    r1 = x - hi.astype(F32)
    mid = r1.astype(BF16)
    lo = (r1 - mid.astype(F32)).astype(BF16)
    return hi, mid, lo


def _mlstm_kernel(q_ref, k_ref, v_ref, o_ref, zif_ref, g_ref, ym_ref, ct_ref, m_ref):
    bsz, L = q_ref.shape[0], q_ref.shape[1]
    dh = M_HEAD_DIM

    @pl.when(pl.program_id(0) == 0)
    def _():
        ct_ref[...] = jnp.zeros(ct_ref.shape, F32)
        m_ref[...] = jnp.zeros(m_ref.shape, F32)

    r_i = lax.broadcasted_iota(jnp.int32, (L, L), 0)
    c_i = lax.broadcasted_iota(jnp.int32, (L, L), 1)
    causal = c_i <= r_i
    triu = jnp.where(r_i <= c_i, 1.0, 0.0).astype(BF16)
    ones = jnp.ones((L, dh), BF16)
    gate_row = lax.broadcasted_iota(jnp.int32, (SUBLANES, L), 0)

    for b in range(bsz):
        gates = zif_ref[b].T[0:SUBLANES, :]
        log_f = jnp.minimum(gates, 0.0) - jnp.log1p(jnp.exp(-jnp.abs(gates)))
        log_f = jnp.where(gate_row >= M_HEADS, log_f, 0.0)
        bcum = sum(_dot(part, triu) for part in _split3(log_f))
        xrow = jnp.where(gate_row < M_HEADS, gates, bcum) * LOG2_E
        xcol = jnp.concatenate([xrow, jnp.zeros((LANES - SUBLANES, L), F32)], axis=0).T

        for h in range(M_HEADS):
            hs = slice(h * dh, (h + 1) * dh)
            st = b * M_HEADS + h
            li_col = xcol[:, h:h + 1]
            b_col = xcol[:, M_HEADS + h:M_HEADS + h + 1]
            li_row = xrow[h:h + 1, :]
            b_row = xrow[M_HEADS + h:M_HEADS + h + 1, :]
            m_prev = m_ref[st:st + 1, 0:1]
            b_last = b_col[L - 1:L, :]
            q = q_ref[b, :, hs]
            k = k_ref[b, :, hs]
            vaug = jnp.concatenate([v_ref[b, :, hs], ones], axis=1)
            ct = ct_ref[st]

            s = _dot_nt(q, k)
            dmat = jnp.where(causal, b_col + (li_row - b_row), -jnp.inf)
            inter = b_col + m_prev
            m_t = jnp.maximum(inter, jnp.max(dmat, axis=1, keepdims=True))
            sw = (s * jnp.exp2(dmat - m_t)).astype(BF16)
            nd = jnp.exp2(inter - m_t) * _dot(q, ct.astype(BF16)) + _dot(sw, vaug)
            hval = nd[:, :dh] / jnp.maximum(jnp.abs(nd[:, dh:]), jnp.exp2(-m_t))
            hn = _rms(hval, g_ref[:, hs])
            ym_ref[b, :, hs] = (hn * jax.nn.sigmoid(o_ref[b, :, hs].astype(F32))).astype(BF16)

            a_col = b_last - b_col + li_col
            m_new = jnp.maximum(b_last + m_prev, jnp.max(a_col, axis=0, keepdims=True))
            kw_t = (k.astype(F32) * jnp.exp2(a_col - m_new)).T.astype(BF16)
            ct_ref[st] = jnp.exp2(b_last + m_prev - m_new) * ct + _dot(kw_t, vaug)
            m_ref[st:st + 1, :] = jnp.broadcast_to(m_new, (1, LANES))


def _mlstm(qm, km, vm, om, zif, m_norm_g):
    bsz, seq, width = qm.shape
    L = MLSTM_CHUNK
    row = lambda c: (0, c, 0)
    return pl.pallas_call(
        _mlstm_kernel,
        grid=(seq // L,),
        in_specs=[pl.BlockSpec((bsz, L, width), row)] * 4
        + [pl.BlockSpec((bsz, L, LANES), row), _const_spec((1, width))],
        out_specs=pl.BlockSpec((bsz, L, width), row),
        out_shape=jax.ShapeDtypeStruct((bsz, seq, width), BF16),
        scratch_shapes=[pltpu.VMEM((bsz * M_HEADS, M_HEAD_DIM, 2 * M_HEAD_DIM), F32),
                        pltpu.VMEM((bsz * M_HEADS, LANES), F32)],
        compiler_params=_cparams(("arbitrary",)),
        name="mlstm",
    )(qm, km, vm, om, zif, m_norm_g)


def _moba_kernel(qt_ref, k_ref, vt_ref, kmean_ref, o_ref, qaug_ref, s_ref, p_ref, acc_ref):
    tq = qt_ref.shape[2]
    blk = MOBA_BLOCK
    nb = k_ref.shape[1] // blk
    dh = A_HEAD_DIM
    i = pl.program_id(2)
    qt = qt_ref[0].astype(F32)
    feat = lax.broadcasted_iota(jnp.int32, (LANES, tq), 0)
    kmean = kmean_ref[0]
    blk_id = lax.broadcasted_iota(jnp.int32, (nb, tq), 0)
    valid = blk_id < i

    q_own, q_aug = [], []
    for hh in range(2):
        in_head = (feat >= hh * dh) & (feat < (hh + 1) * dh)
        qh = jnp.where(in_head, qt, 0.0).astype(BF16)
        g = jnp.where(valid, _dot(kmean, qh), -jnp.inf)
        sel = jnp.zeros(g.shape, jnp.bool_)
        for _ in range(MOBA_TOPK):
            mx = jnp.max(g, axis=0, keepdims=True)
            idx = jnp.min(jnp.where(g == mx, blk_id, nb), axis=0, keepdims=True)
            pick = (blk_id == idx) & valid
            sel = sel | pick
            g = jnp.where(pick, -jnp.inf, g)
        bias = jnp.where(sel, 0.0, NEG_BIG).astype(BF16)
        q_own.append(qh)
        q_aug.append(jnp.concatenate([qh, bias, jnp.zeros((LANES - nb, tq), BF16)], axis=0))
    qaug_ref[...] = jnp.concatenate(q_aug, axis=1)
    lane_k = lax.broadcasted_iota(jnp.int32, (blk, LANES), 1)

    def fold8(x, op):
        parts = [x[r:r + SUBLANES] for r in range(0, x.shape[0], SUBLANES)]
        while len(parts) > 1:
            parts = [op(a, b) for a, b in zip(parts[0::2], parts[1::2])]
        return parts[0]

    def put_scores(j, slot):
        st = pl.multiple_of(j * blk, blk)
        onehot = jnp.where(lane_k == j, 1.0, 0.0).astype(BF16)
        k_aug = jnp.concatenate([k_ref[0, pl.ds(st, blk), :], onehot], axis=1)
        sj = _dot(k_aug, qaug_ref[...])
        s_ref[slot] = sj
        return jnp.max(fold8(sj, jnp.maximum), axis=0, keepdims=True)

    def softmax_into(s_slot, p_slot, m_new):
        rows = 4 * SUBLANES
        for r in range(0, blk, rows):
            x = (s_ref[s_slot, r:r + rows, :] - m_new).astype(BF16)
            p_ref[p_slot, r:r + rows, :] = jnp.exp2(x)

    ones_rows = jnp.ones((ACC_ROWS - dh, blk), BF16)

    def add_pv(slot, vidx, alpha):
        st = pl.multiple_of(vidx * blk, blk)
        for hh in range(2):
            qs = slice(hh * tq, (hh + 1) * tq)
            v_aug = jnp.concatenate([vt_ref[0, hh * dh:(hh + 1) * dh, pl.ds(st, blk)], ones_rows], axis=0)
            acc_ref[hh] = alpha[:, qs] * acc_ref[hh] + _dot(v_aug, p_ref[slot, :, qs])

    start = pl.multiple_of(i * blk, blk)
    s = _dot(k_ref[0, pl.ds(start, blk), :], jnp.concatenate(q_own, axis=1))
    key_i = lax.broadcasted_iota(jnp.int32, (blk, 2 * tq), 0)
    qry_i = lax.broadcasted_iota(jnp.int32, (blk, 2 * tq), 1) & (tq - 1)
    s_ref[1] = jnp.where(key_i <= qry_i, s, NEG_BIG)
    m0 = jnp.max(fold8(s_ref[1], jnp.maximum), axis=0, keepdims=True)
    softmax_into(1, 1, m0)
    acc_ref[...] = jnp.zeros(acc_ref.shape, F32)
    cmax0 = put_scores(0, 0)

    def step(j, slot, state):
        cmax, alpha_pend, vidx, m = state
        cmax_next = put_scores(jnp.minimum(j + 1, nb - 1), 1 - slot)
        add_pv(1 - slot, vidx, alpha_pend)
        m_new = jnp.maximum(m, cmax)
        softmax_into(slot, slot, m_new)
        return cmax_next, jnp.exp2(m - m_new), j, m_new

    def run(first_block, n_trips, unroll, state):
        def body(t, st):
            for u in range(unroll):
                st = step(first_block + unroll * t + u, u % 2, st)
            return st
        return lax.fori_loop(0, n_trips, body, state)

    long_trips = i // MOBA_UNROLL
    done = long_trips * MOBA_UNROLL
    state = run(0, long_trips, MOBA_UNROLL, (cmax0, jnp.ones_like(m0), i, m0))
    state = run(done, (i - done + 1) // 2, 2, state)
    _, alpha_pend, vidx, _ = state
    add_pv(1, vidx, alpha_pend)
    out = jnp.concatenate([acc_ref[hh, :dh, :] / acc_ref[hh, dh:dh + 1, :] for hh in range(2)],
                          axis=0)
    o_ref[0] = out.T.astype(BF16)


def _moba(qat, ka, vat, kmean):
    bsz, width, seq = qat.shape
    tq = MOBA_BLOCK
    npair = width // LANES
    return pl.pallas_call(
        _moba_kernel,
        grid=(bsz, npair, seq // tq),
        in_specs=[pl.BlockSpec((1, LANES, tq), lambda b, p, i: (b, p, i)),
                  pl.BlockSpec((1, seq, LANES), lambda b, p, i: (b, 0, p)),
                  pl.BlockSpec((1, LANES, seq), lambda b, p, i: (b, p, 0)),
                  pl.BlockSpec((1, kmean.shape[1], LANES), lambda b, p, i: (b, 0, p))],
        out_specs=pl.BlockSpec((1, tq, LANES), lambda b, p, i: (b, i, p)),
        out_shape=jax.ShapeDtypeStruct((bsz, seq, width), BF16),
        scratch_shapes=[pltpu.VMEM((2 * LANES, 2 * tq), BF16),
                        pltpu.VMEM((2, MOBA_BLOCK, 2 * tq), F32),
                        pltpu.VMEM((2, MOBA_BLOCK, 2 * tq), BF16),
                        pltpu.VMEM((2, ACC_ROWS, tq), F32)],
        compiler_params=_cparams(("arbitrary", "arbitrary", "arbitrary")),
        name="moba",
    )(qat, ka, vat, kmean)


def _tail_kernel(x_ref, ym_ref, ya_ref, sga_ref, sgb_ref, mod_ref, g2_ref, gf_ref,
                 pm_ref, pa_ref, wo_ref, wg_ref, wu_ref, wd_ref, o_ref, *, final_norm):
    gate1 = mod_ref[0, 2:3, :]
    sh = mod_ref[0, 3:4, :]
    sc = mod_ref[0, 4:5, :]
    gate = mod_ref[0, 5:6, :]

    merged = (sga_ref[0].astype(F32) * _dot(ym_ref[0], pm_ref[...])
              + sgb_ref[0].astype(F32) * _dot(ya_ref[0], pa_ref[...]))
    x = x_ref[0] + gate1 * _dot(merged.astype(BF16), wo_ref[...])

    h = (_rms(x, g2_ref[...]) * (1.0 + sc) + sh).astype(BF16)
    d_ff = wg_ref.shape[1]
    step = 256
    acc = jnp.zeros(x.shape, F32)
    for c0 in range(0, d_ff, step):
        a = _dot(h, wg_ref[:, c0:c0 + step])
        u = _dot(h, wu_ref[:, c0:c0 + step])
        f = (a * jax.nn.sigmoid(a) * u).astype(BF16)
        acc = acc + _dot(f, wd_ref[c0:c0 + step, :])
    y = x + gate * acc
    o_ref[0] = _rms(y, gf_ref[...]) if final_norm else y


def _tail(x, ym, ya, sga, sgb, mod3, norm2_g, normf_g, weights, final_norm):
    bsz, seq, d = x.shape
    tm = ROW_TILE
    row = lambda b, i: (b, i, 0)
    return pl.pallas_call(
        functools.partial(_tail_kernel, final_norm=final_norm),
        grid=(bsz, seq // tm),
        in_specs=[pl.BlockSpec((1, tm, d), row),
                  pl.BlockSpec((1, tm, ym.shape[2]), row),
                  pl.BlockSpec((1, tm, ya.shape[2]), row),
                  pl.BlockSpec((1, tm, d), row),
                  pl.BlockSpec((1, tm, d), row),
                  pl.BlockSpec((1, N_MOD, d), lambda b, i: (b, 0, 0)),
                  _const_spec((1, d)), _const_spec((1, d)),
                  *[_const_spec(w.shape) for w in weights]],
        out_specs=pl.BlockSpec((1, tm, d), row),
        out_shape=jax.ShapeDtypeStruct((bsz, seq, d), F32),
        compiler_params=_cparams(("arbitrary", "arbitrary")),
        name="merge_ffn_out",
    )(x, ym, ya, sga, sgb, mod3, norm2_g, normf_g, *weights)


def _in_proj_params(w_in, b_in, layer):
    m_end = 4 * M_WIDTH
    if_end = m_end + 2 * M_HEADS
    pad = ((0, 0), (0, LANES - 2 * M_HEADS))
    w = w_in[layer]
    b = b_in[layer].reshape(1, -1)
    weights = (w[:, :m_end].astype(BF16), w[:, if_end:].astype(BF16),
               jnp.pad(w[:, m_end:if_end], pad).astype(BF16))
    biases = (b[:, :m_end], b[:, if_end:], jnp.pad(b[:, m_end:if_end], pad))
    return weights, biases


def kernel(x, c, positions, ada_w, ada_b, norm1_g, norm2_g, normf_g, w_in, b_in, conv_w, conv_b,
           m_norm_g, p_mlstm, p_moba, w_out, w_gate, w_up, w_down):
    bsz, seq, d = x.shape
    depth = ada_w.shape[0]
    n_blocks = seq // MOBA_BLOCK
    cos_t, sin_t = _rope_tables(positions)
    for l in range(depth):
        mod3 = _ada_mod(c, ada_w, ada_b, l).reshape(bsz, N_MOD, d)
        weights, biases = _in_proj_params(w_in, b_in, l)
        (qm, km, vm, om, qa, ka, va, sga, sgb, zif, kmean_tiles) = _in_proj(
            x, mod3, norm1_g[l].reshape(1, d), weights, biases, conv_w[l],
            conv_b[l].reshape(1, -1), cos_t, sin_t)
        ym = _mlstm(qm, km, vm, om, zif, m_norm_g[l].reshape(1, -1))
        kmean = kmean_tiles[:, :ROW_TILE // MOBA_BLOCK].reshape(bsz, n_blocks, A_WIDTH)
        ya = _moba(qa, ka, va, kmean.astype(BF16))
        tail_weights = [w[l].astype(BF16) for w in (p_mlstm, p_moba, w_out, w_gate, w_up, w_down)]
        x = _tail(x, ym, ya, sga, sgb, mod3, norm2_g[l].reshape(1, d), normf_g.reshape(1, d),
                  tail_weights, final_norm=(l == depth - 1))
    return x
```

```python
import functools

import jax
import jax.numpy as jnp
from jax import lax
from jax.experimental import pallas as pl
from jax.experimental.pallas import tpu as pltpu

F32 = jnp.float32
BF16 = jnp.bfloat16

M_HEADS = 4
M_HEAD_DIM = 128
M_WIDTH = M_HEADS * M_HEAD_DIM
M_CONV = 4
A_HEADS = 8
A_HEAD_DIM = 64
A_WIDTH = A_HEADS * A_HEAD_DIM
MOBA_BLOCK = 256
MOBA_TOPK = 3
ROPE_THETA = 10000.0
N_MOD = 6
RMS_EPS = 1e-6

LANES = 128
SUBLANES = 8
VMEM_LIMIT = 56 * 1024 * 1024

ROW_TILE = 512
MLSTM_CHUNK = 256
NEG_BIG = -1e30
LOG2_E = 1.4426950408889634
MOBA_UNROLLS = (8, 4)
ACC_ROWS = A_HEAD_DIM + 16

OFF_MQ, OFF_MK, OFF_MV, OFF_MO = 0, 512, 1024, 1536
OFF_AQ, OFF_AK, OFF_AV = 2048, 2560, 3072
OFF_GA, OFF_GB = 3584, 4608
OFF_IF = 5632
D_IN_PACKED = 5760


def _cparams(sem, flags=None):
    return pltpu.CompilerParams(dimension_semantics=sem, vmem_limit_bytes=VMEM_LIMIT, flags=flags)


def _const_spec(shape):
    nd = len(shape)
    return pl.BlockSpec(shape, lambda *_: (0,) * nd, pipeline_mode=pl.Buffered(1))


def _rms(x, g):
    ms = jnp.mean(x * x, axis=-1, keepdims=True)
    return x * lax.rsqrt(ms + RMS_EPS) * g


def _dot(a, b):
    return jnp.dot(a, b, preferred_element_type=F32)


def _dot_nt(a, b):
    return lax.dot_general(a, b, (((1,), (1,)), ((), ())), preferred_element_type=F32)


def _ada_kernel(c_ref, w_ref, b_ref, o_ref):
    c = c_ref[...]
    ca = c * jax.nn.sigmoid(c)
    o_ref[...] = jnp.dot(ca, w_ref[...], precision=lax.Precision.HIGHEST,
                         preferred_element_type=F32) + b_ref[...]


def _ada_mod(c, ada_w, ada_b, layer):
    bsz, d = c.shape
    depth, _, n = ada_w.shape
    tn = d
    return pl.pallas_call(
        _ada_kernel,
        grid=(n // tn,),
        in_specs=[pl.BlockSpec((bsz, d), lambda j: (0, 0)),
                  pl.BlockSpec((None, d, tn), lambda j: (layer, 0, j)),
                  pl.BlockSpec((None, 1, tn), lambda j: (layer, 0, j))],
        out_specs=pl.BlockSpec((bsz, tn), lambda j: (0, j)),
        out_shape=jax.ShapeDtypeStruct((bsz, n), F32),
        compiler_params=_cparams(("arbitrary",)),
        name="ada_mod",
    )(c, ada_w, ada_b.reshape(depth, 1, n))


def _rope_kernel(pos_ref, freq_ref, cos_ref, sin_ref):
    ang = freq_ref[...] * pos_ref[0].astype(F32)
    c = jnp.cos(ang)
    s = jnp.sin(ang)
    cos_ref[0] = jnp.concatenate([c, c, c, c], axis=0).T
    sin_ref[0] = jnp.concatenate([-s, s, -s, s], axis=0).T


def _rope_tables(positions):
    bsz, seq = positions.shape
    half = A_HEAD_DIM // 2
    inv_freq = (ROPE_THETA ** (-jnp.arange(half, dtype=F32) / half)).reshape(half, 1)
    ts = 2048
    out = jax.ShapeDtypeStruct((bsz, seq, LANES), F32)
    return pl.pallas_call(
        _rope_kernel,
        grid=(bsz, seq // ts),
        in_specs=[pl.BlockSpec((1, 1, ts), lambda b, i: (b, 0, i)),
                  pl.BlockSpec((half, 1), lambda b, i: (0, 0))],
        out_specs=[pl.BlockSpec((1, ts, LANES), lambda b, i: (b, i, 0))] * 2,
        out_shape=[out, out],
        compiler_params=_cparams(("arbitrary", "arbitrary")),
        name="rope_table",
    )(positions.reshape(bsz, 1, seq), inv_freq)


def _in_proj_kernel(x_ref, mod_ref, g_ref, wm_ref, wa_ref, wif_ref, bm_ref, ba_ref, bif_ref,
                    cw_ref, cb_ref, cos_ref, sin_ref,
                    qm_ref, km_ref, vm_ref, om_ref, qa_ref, ka_ref, va_ref,
                    sga_ref, sgb_ref, zif_ref, kmean_ref, hb_ref, zbuf_ref):
    tm = x_ref.shape[1]
    it = pl.program_id(1)

    x = x_ref[0]
    sh = mod_ref[0, 0:1, :]
    sc = mod_ref[0, 1:2, :]
    hb_ref[...] = (_rms(x, g_ref[...]) * (1.0 + sc) + sh).astype(BF16)

    def proj(off, width):
        for base, w_ref, b_ref in ((OFF_IF, wif_ref, bif_ref), (OFF_AQ, wa_ref, ba_ref), (OFF_MQ, wm_ref, bm_ref)):
            if off >= base:
                lo = off - base
                return _dot_nt(hb_ref[...], w_ref[lo:lo + width, :]) + b_ref[:, lo:lo + width]

    @pl.when(it == 0)
    def _():
        zbuf_ref[0:SUBLANES, :] = jnp.zeros((SUBLANES, zbuf_ref.shape[1]), F32)

    @pl.when(it > 0)
    def _():
        zbuf_ref[0:SUBLANES, :] = zbuf_ref[tm:tm + SUBLANES, :]

    zbuf_ref[SUBLANES:SUBLANES + tm, 0:M_WIDTH] = proj(OFF_MQ, M_WIDTH)
    zbuf_ref[SUBLANES:SUBLANES + tm, M_WIDTH:2 * M_WIDTH] = proj(OFF_MK, M_WIDTH)

    def conv_silu(part):
        width = 2 * LANES
        cols = slice(part * width, (part + 1) * width)
        acc = jnp.broadcast_to(cb_ref[:, cols], (tm, width))
        for j in range(M_CONV):
            start = SUBLANES - (M_CONV - 1) + j
            acc = acc + zbuf_ref[start:start + tm, cols] * cw_ref[j:j + 1, cols]
        y = acc * jax.nn.sigmoid(acc)
        if part * width < M_WIDTH:
            qm_ref[0, :, cols] = y.astype(BF16)
        else:
            kcols = slice(part * width - M_WIDTH, (part + 1) * width - M_WIDTH)
            km_ref[0, :, kcols] = (y * M_HEAD_DIM ** -0.5).astype(BF16)

    conv_silu(0)
    vm_ref[0] = proj(OFF_MV, M_WIDTH).astype(BF16)
    conv_silu(1)
    om_ref[0] = proj(OFF_MO, M_WIDTH).astype(BF16)
    conv_silu(2)

    def rope(z):
        outs = []
        for p in range(A_WIDTH // LANES):
            zp = z[:, p * LANES:(p + 1) * LANES]
            lane = lax.broadcasted_iota(jnp.int32, zp.shape, 1)
            first_half = (lane % A_HEAD_DIM) < (A_HEAD_DIM // 2)
            rot = jnp.where(first_half,
                            pltpu.roll(zp, LANES - A_HEAD_DIM // 2, 1),
                            pltpu.roll(zp, A_HEAD_DIM // 2, 1))
            outs.append(zp * cos_ref[0] + rot * sin_ref[0])
        return outs

    q_parts = rope(proj(OFF_AQ, A_WIDTH))
    for p, qp in enumerate(q_parts):
        qa_ref[0, p * LANES:(p + 1) * LANES, :] = (qp * (LOG2_E * A_HEAD_DIM ** -0.5)).T.astype(BF16)
    conv_silu(3)

    k_parts = rope(proj(OFF_AK, A_WIDTH))
    kmean_ref[0] = jnp.zeros(kmean_ref.shape[1:], F32)
    for p, kp in enumerate(k_parts):
        ka_ref[0, :, p * LANES:(p + 1) * LANES] = kp.astype(BF16)
        for blk in range(tm // MOBA_BLOCK):
            kmean_ref[0, blk:blk + 1, p * LANES:(p + 1) * LANES] = jnp.mean(
                kp[blk * MOBA_BLOCK:(blk + 1) * MOBA_BLOCK], axis=0, keepdims=True)

    va = proj(OFF_AV, A_WIDTH)
    for p in range(A_WIDTH // LANES):
        va_ref[0, p * LANES:(p + 1) * LANES, :] = va[:, p * LANES:(p + 1) * LANES].T.astype(BF16)

    d = x.shape[1]
    for off, out_ref in ((OFF_GA, sga_ref), (OFF_GB, sgb_ref)):
        for c0 in range(0, d, 512):
            out_ref[0, :, c0:c0 + 512] = jax.nn.sigmoid(proj(off + c0, 512)).astype(BF16)

    zif_ref[0] = proj(OFF_IF, LANES)


def _in_proj(x, mod3, norm_g, weights, biases, conv_w, conv_b, cos_t, sin_t):
    bsz, seq, d = x.shape
    tm = ROW_TILE
    nt = seq // tm
    row = lambda b, i: (b, i, 0)

    def rows(width, dtype):
        return (pl.BlockSpec((1, tm, width), row), jax.ShapeDtypeStruct((bsz, seq, width), dtype))

    cols = (pl.BlockSpec((1, A_WIDTH, tm), lambda b, i: (b, 0, i)),
            jax.ShapeDtypeStruct((bsz, A_WIDTH, seq), BF16))
    outs = ([rows(M_WIDTH, BF16)] * 4 + [cols, rows(A_WIDTH, BF16), cols]
            + [rows(d, BF16)] * 2 + [rows(LANES, F32)])
    outs.append((pl.BlockSpec((1, SUBLANES, A_WIDTH), lambda b, i: (b * nt + i, 0, 0)),
                 jax.ShapeDtypeStruct((bsz * nt, SUBLANES, A_WIDTH), F32)))
    return pl.pallas_call(
        _in_proj_kernel,
        grid=(bsz, nt),
        in_specs=[pl.BlockSpec((1, tm, d), row),
                  pl.BlockSpec((1, N_MOD, d), lambda b, i: (b, 0, 0)),
                  _const_spec((1, d)),
                  *[_const_spec(t.shape) for t in (*weights, *biases)],
                  _const_spec(conv_w.shape),
                  _const_spec(conv_b.shape),
                  pl.BlockSpec((1, tm, LANES), row),
                  pl.BlockSpec((1, tm, LANES), row)],
        out_specs=[o[0] for o in outs],
        out_shape=[o[1] for o in outs],
        scratch_shapes=[pltpu.VMEM((tm, d), BF16),
                        pltpu.VMEM((tm + SUBLANES, 2 * M_WIDTH), F32)],
        compiler_params=_cparams(("arbitrary", "arbitrary")),
        name="in_proj",
    )(x, mod3, norm_g, *weights, *biases, conv_w, conv_b, cos_t, sin_t)


def _split3(x):
    hi = x.astype(BF16)
    r1 = x - hi.astype(F32)
    mid = r1.astype(BF16)
    lo = (r1 - mid.astype(F32)).astype(BF16)
    return hi, mid, lo


def _mlstm_kernel(q_ref, k_ref, v_ref, o_ref, zif_ref, g_ref, ym_ref, ct_ref, m_ref):
    bsz, L = q_ref.shape[0], q_ref.shape[1]
    dh = M_HEAD_DIM

    @pl.when(pl.program_id(0) == 0)
    def _():
        ct_ref[...] = jnp.zeros(ct_ref.shape, F32)
        m_ref[...] = jnp.zeros(m_ref.shape, F32)

    r_i = lax.broadcasted_iota(jnp.int32, (L, L), 0)
    c_i = lax.broadcasted_iota(jnp.int32, (L, L), 1)
    causal = c_i <= r_i
    triu = jnp.where(r_i <= c_i, 1.0, 0.0).astype(BF16)
    ones = jnp.ones((L, dh), BF16)
    gate_row = lax.broadcasted_iota(jnp.int32, (SUBLANES, L), 0)

    for b in range(bsz):
        gates = zif_ref[b].T[0:SUBLANES, :]
        log_f = jnp.minimum(gates, 0.0) - jnp.log1p(jnp.exp(-jnp.abs(gates)))
        log_f = jnp.where(gate_row >= M_HEADS, log_f, 0.0)
        bcum = sum(_dot(part, triu) for part in _split3(log_f))
        xrow = jnp.where(gate_row < M_HEADS, gates, bcum) * LOG2_E
        xcol = jnp.concatenate([xrow, jnp.zeros((LANES - SUBLANES, L), F32)], axis=0).T

        for h in range(M_HEADS):
            hs = slice(h * dh, (h + 1) * dh)
            st = b * M_HEADS + h
            li_col = xcol[:, h:h + 1]
            b_col = xcol[:, M_HEADS + h:M_HEADS + h + 1]
            li_row = xrow[h:h + 1, :]
            b_row = xrow[M_HEADS + h:M_HEADS + h + 1, :]
            m_prev = m_ref[st:st + 1, 0:1]
            b_last = b_col[L - 1:L, :]
            q = q_ref[b, :, hs]
            k = k_ref[b, :, hs]
            vaug = jnp.concatenate([v_ref[b, :, hs], ones], axis=1)
            ct = ct_ref[st]

            s = _dot_nt(q, k)
            dmat = jnp.where(causal, b_col + (li_row - b_row), -jnp.inf)
            inter = b_col + m_prev
            m_t = jnp.maximum(inter, jnp.max(dmat, axis=1, keepdims=True))
            sw = (s * jnp.exp2(dmat - m_t)).astype(BF16)
            nd = jnp.exp2(inter - m_t) * _dot(q, ct.astype(BF16)) + _dot(sw, vaug)
            hval = nd[:, :dh] / jnp.maximum(jnp.abs(nd[:, dh:]), jnp.exp2(-m_t))
            hn = _rms(hval, g_ref[:, hs])
            ym_ref[b, :, hs] = (hn * jax.nn.sigmoid(o_ref[b, :, hs].astype(F32))).astype(BF16)

            a_col = b_last - b_col + li_col
            m_new = jnp.maximum(b_last + m_prev, jnp.max(a_col, axis=0, keepdims=True))
            kw_t = (k.astype(F32) * jnp.exp2(a_col - m_new)).T.astype(BF16)
            ct_ref[st] = jnp.exp2(b_last + m_prev - m_new) * ct + _dot(kw_t, vaug)
            m_ref[st:st + 1, :] = jnp.broadcast_to(m_new, (1, LANES))


def _mlstm(qm, km, vm, om, zif, m_norm_g):
    bsz, seq, width = qm.shape
    L = MLSTM_CHUNK
    row = lambda c: (0, c, 0)
    return pl.pallas_call(
        _mlstm_kernel,
        grid=(seq // L,),
        in_specs=[pl.BlockSpec((bsz, L, width), row)] * 4
        + [pl.BlockSpec((bsz, L, LANES), row), _const_spec((1, width))],
        out_specs=pl.BlockSpec((bsz, L, width), row),
        out_shape=jax.ShapeDtypeStruct((bsz, seq, width), BF16),
        scratch_shapes=[pltpu.VMEM((bsz * M_HEADS, M_HEAD_DIM, 2 * M_HEAD_DIM), F32),
                        pltpu.VMEM((bsz * M_HEADS, LANES), F32)],
        compiler_params=_cparams(("arbitrary",)),
        name="mlstm",
    )(qm, km, vm, om, zif, m_norm_g)


def _moba_kernel(qt_ref, k_ref, vt_ref, kmean_ref, o_ref, qaug_ref, s_ref, p_ref, acc_ref):
    tq = qt_ref.shape[2]
    blk = MOBA_BLOCK
    nb = k_ref.shape[1] // blk
    dh = A_HEAD_DIM
    i = pl.program_id(2)
    qt = qt_ref[0].astype(F32)
    feat = lax.broadcasted_iota(jnp.int32, (LANES, tq), 0)
    kmean = kmean_ref[0]
    blk_id = lax.broadcasted_iota(jnp.int32, (nb, tq), 0)
    valid = blk_id < i

    q_own, q_aug = [], []
    for hh in range(2):
        in_head = (feat >= hh * dh) & (feat < (hh + 1) * dh)
        qh = jnp.where(in_head, qt, 0.0).astype(BF16)
        g = jnp.where(valid, _dot(kmean, qh), -jnp.inf)
        sel = jnp.zeros(g.shape, jnp.bool_)
        for _ in range(MOBA_TOPK):
            mx = jnp.max(g, axis=0, keepdims=True)
            idx = jnp.min(jnp.where(g == mx, blk_id, nb), axis=0, keepdims=True)
            pick = (blk_id == idx) & valid
            sel = sel | pick
            g = jnp.where(pick, -jnp.inf, g)
        bias = jnp.where(sel, 0.0, NEG_BIG).astype(BF16)
        q_own.append(qh)
        q_aug.append(jnp.concatenate([qh, bias, jnp.zeros((LANES - nb, tq), BF16)], axis=0))
    qaug_ref[...] = jnp.concatenate(q_aug, axis=1)
    lane_k = lax.broadcasted_iota(jnp.int32, (blk, LANES), 1)

    def fold8(x, op):
        parts = [x[r:r + SUBLANES] for r in range(0, x.shape[0], SUBLANES)]
        while len(parts) > 1:
            parts = [op(a, b) for a, b in zip(parts[0::2], parts[1::2])]
        return parts[0]

    def put_scores(j, slot):
        st = pl.multiple_of(j * blk, blk)
        onehot = jnp.where(lane_k == j, 1.0, 0.0).astype(BF16)
        k_aug = jnp.concatenate([k_ref[0, pl.ds(st, blk), :], onehot], axis=1)
        sj = _dot(k_aug, qaug_ref[...])
        s_ref[slot] = sj
        return jnp.max(fold8(sj, jnp.maximum), axis=0, keepdims=True)

    def softmax_into(s_slot, p_slot, m_new):
        rows = 4 * SUBLANES
        for r in range(0, blk, rows):
            x = (s_ref[s_slot, r:r + rows, :] - m_new).astype(BF16)
            p_ref[p_slot, r:r + rows, :] = jnp.exp2(x)

    ones_rows = jnp.ones((ACC_ROWS - dh, blk), BF16)

    def add_pv(slot, vidx, alpha):
        st = pl.multiple_of(vidx * blk, blk)
        for hh in range(2):
            qs = slice(hh * tq, (hh + 1) * tq)
            v_aug = jnp.concatenate([vt_ref[0, hh * dh:(hh + 1) * dh, pl.ds(st, blk)], ones_rows], axis=0)
            acc_ref[hh] = alpha[:, qs] * acc_ref[hh] + _dot(v_aug, p_ref[slot, :, qs])

    start = pl.multiple_of(i * blk, blk)
    s = _dot(k_ref[0, pl.ds(start, blk), :], jnp.concatenate(q_own, axis=1))
    key_i = lax.broadcasted_iota(jnp.int32, (blk, 2 * tq), 0)
    qry_i = lax.broadcasted_iota(jnp.int32, (blk, 2 * tq), 1) & (tq - 1)
    s_ref[1] = jnp.where(key_i <= qry_i, s, NEG_BIG)
    m0 = jnp.max(fold8(s_ref[1], jnp.maximum), axis=0, keepdims=True)
    softmax_into(1, 1, m0)
    acc_ref[...] = jnp.zeros(acc_ref.shape, F32)
    cmax0 = put_scores(0, 0)

    def step(j, slot, state):
        cmax, alpha_pend, vidx, m = state
        cmax_next = put_scores(jnp.minimum(j + 1, nb - 1), 1 - slot)
        add_pv(1 - slot, vidx, alpha_pend)
        m_new = jnp.maximum(m, cmax)
        softmax_into(slot, slot, m_new)
        return cmax_next, jnp.exp2(m - m_new), j, m_new

    def run(first_block, n_trips, unroll, state):
        def body(t, st):
            for u in range(unroll):
                st = step(first_block + unroll * t + u, u % 2, st)
            return st
        return lax.fori_loop(0, n_trips, body, state)

    state = (cmax0, jnp.ones_like(m0), i, m0)
    done = 0
    for unroll in MOBA_UNROLLS:
        trips = (i - done) // unroll
        state = run(done, trips, unroll, state)
        done = done + trips * unroll
    state = run(done, (i - done + 1) // 2, 2, state)
    _, alpha_pend, vidx, _ = state
    add_pv(1, vidx, alpha_pend)
    out = jnp.concatenate([acc_ref[hh, :dh, :] / acc_ref[hh, dh:dh + 1, :] for hh in range(2)],
                          axis=0)
    o_ref[0] = out.T.astype(BF16)


def _moba(qat, ka, vat, kmean):
    bsz, width, seq = qat.shape
    tq = MOBA_BLOCK
    npair = width // LANES
    return pl.pallas_call(
        _moba_kernel,
        grid=(bsz, npair, seq // tq),
        in_specs=[pl.BlockSpec((1, LANES, tq), lambda b, p, i: (b, p, i)),
                  pl.BlockSpec((1, seq, LANES), lambda b, p, i: (b, 0, p)),
                  pl.BlockSpec((1, LANES, seq), lambda b, p, i: (b, p, 0)),
                  pl.BlockSpec((1, kmean.shape[1], LANES), lambda b, p, i: (b, 0, p))],
        out_specs=pl.BlockSpec((1, tq, LANES), lambda b, p, i: (b, i, p)),
        out_shape=jax.ShapeDtypeStruct((bsz, seq, width), BF16),
        scratch_shapes=[pltpu.VMEM((2 * LANES, 2 * tq), BF16),
                        pltpu.VMEM((2, MOBA_BLOCK, 2 * tq), F32),
                        pltpu.VMEM((2, MOBA_BLOCK, 2 * tq), BF16),
                        pltpu.VMEM((2, ACC_ROWS, tq), F32)],
        compiler_params=_cparams(("arbitrary", "arbitrary", "arbitrary")),
        name="moba",
    )(qat, ka, vat, kmean)


def _tail_kernel(x_ref, ym_ref, ya_ref, sga_ref, sgb_ref, mod_ref, g2_ref, gf_ref,
                 pm_ref, pa_ref, wo_ref, wg_ref, wu_ref, wd_ref, o_ref, *, final_norm):
    gate1 = mod_ref[0, 2:3, :]
    sh = mod_ref[0, 3:4, :]
    sc = mod_ref[0, 4:5, :]
    gate = mod_ref[0, 5:6, :]

    merged = (sga_ref[0].astype(F32) * _dot(ym_ref[0], pm_ref[...])
              + sgb_ref[0].astype(F32) * _dot(ya_ref[0], pa_ref[...]))
    x = x_ref[0] + gate1 * _dot(merged.astype(BF16), wo_ref[...])

    h = (_rms(x, g2_ref[...]) * (1.0 + sc) + sh).astype(BF16)
    d_ff = wg_ref.shape[1]
    step = 256
    acc = jnp.zeros(x.shape, F32)
    for c0 in range(0, d_ff, step):
        a = _dot(h, wg_ref[:, c0:c0 + step])
        u = _dot(h, wu_ref[:, c0:c0 + step])
        f = (a * jax.nn.sigmoid(a) * u).astype(BF16)
        acc = acc + _dot(f, wd_ref[c0:c0 + step, :])
    y = x + gate * acc
    o_ref[0] = _rms(y, gf_ref[...]) if final_norm else y


def _tail(x, ym, ya, sga, sgb, mod3, norm2_g, normf_g, weights, final_norm):
    bsz, seq, d = x.shape
    tm = ROW_TILE
    row = lambda b, i: (b, i, 0)
    return pl.pallas_call(
        functools.partial(_tail_kernel, final_norm=final_norm),
        grid=(bsz, seq // tm),
        in_specs=[pl.BlockSpec((1, tm, d), row),
                  pl.BlockSpec((1, tm, ym.shape[2]), row),
                  pl.BlockSpec((1, tm, ya.shape[2]), row),
                  pl.BlockSpec((1, tm, d), row),
                  pl.BlockSpec((1, tm, d), row),
                  pl.BlockSpec((1, N_MOD, d), lambda b, i: (b, 0, 0)),
                  _const_spec((1, d)), _const_spec((1, d)),
                  *[_const_spec(w.shape) for w in weights]],
        out_specs=pl.BlockSpec((1, tm, d), row),
        out_shape=jax.ShapeDtypeStruct((bsz, seq, d), F32),
        compiler_params=_cparams(("arbitrary", "arbitrary")),
        name="merge_ffn_out",
    )(x, ym, ya, sga, sgb, mod3, norm2_g, normf_g, *weights)


def _in_proj_params(w_in, b_in, layer):
    m_end = 4 * M_WIDTH
    if_end = m_end + 2 * M_HEADS
    n_pad = LANES - 2 * M_HEADS
    wt = jnp.swapaxes(w_in[layer], 0, 1)
    b = b_in[layer].reshape(1, -1)
    weights = (wt[:m_end].astype(BF16), wt[if_end:].astype(BF16),
               jnp.pad(wt[m_end:if_end], ((0, n_pad), (0, 0))).astype(BF16))
    biases = (b[:, :m_end], b[:, if_end:], jnp.pad(b[:, m_end:if_end], ((0, 0), (0, n_pad))))
    return weights, biases


def kernel(x, c, positions, ada_w, ada_b, norm1_g, norm2_g, normf_g, w_in, b_in, conv_w, conv_b,
           m_norm_g, p_mlstm, p_moba, w_out, w_gate, w_up, w_down):
    bsz, seq, d = x.shape
    depth = ada_w.shape[0]
    n_blocks = seq // MOBA_BLOCK
    cos_t, sin_t = _rope_tables(positions)
    for l in range(depth):
        mod3 = _ada_mod(c, ada_w, ada_b, l).reshape(bsz, N_MOD, d)
        weights, biases = _in_proj_params(w_in, b_in, l)
        (qm, km, vm, om, qa, ka, va, sga, sgb, zif, kmean_tiles) = _in_proj(
            x, mod3, norm1_g[l].reshape(1, d), weights, biases, conv_w[l],
            conv_b[l].reshape(1, -1), cos_t, sin_t)
        ym = _mlstm(qm, km, vm, om, zif, m_norm_g[l].reshape(1, -1))
        kmean = kmean_tiles[:, :ROW_TILE // MOBA_BLOCK].reshape(bsz, n_blocks, A_WIDTH)
        ya = _moba(qa, ka, va, kmean.astype(BF16))
        tail_weights = [w[l].astype(BF16) for w in (p_mlstm, p_moba, w_out, w_gate, w_up, w_down)]
        x = _tail(x, ym, ya, sga, sgb, mod3, norm2_g[l].reshape(1, d), normf_g.reshape(1, d),
                  tail_weights, final_norm=(l == depth - 1))
    return x
```

```python
import functools

import jax
import jax.numpy as jnp
from jax import lax
from jax.experimental import pallas as pl
from jax.experimental.pallas import tpu as pltpu

F32 = jnp.float32
BF16 = jnp.bfloat16

M_HEADS = 4
M_HEAD_DIM = 128
M_WIDTH = M_HEADS * M_HEAD_DIM
M_CONV = 4
A_HEADS = 8
A_HEAD_DIM = 64
A_WIDTH = A_HEADS * A_HEAD_DIM
MOBA_BLOCK = 256
MOBA_TOPK = 3
ROPE_THETA = 10000.0
N_MOD = 6
RMS_EPS = 1e-6

LANES = 128
SUBLANES = 8
VMEM_LIMIT = 56 * 1024 * 1024

ROW_TILE = 512
MLSTM_CHUNK = 256
NEG_BIG = -1e30
LOG2_E = 1.4426950408889634
MOBA_TILES = 4
MOBA_UNROLLS = (8, 4)
ACC_ROWS = A_HEAD_DIM + 16

OFF_MQ, OFF_MK, OFF_MV, OFF_MO = 0, 512, 1024, 1536
OFF_AQ, OFF_AK, OFF_AV = 2048, 2560, 3072
OFF_GA, OFF_GB = 3584, 4608
OFF_IF = 5632
D_IN_PACKED = 5760


def _cparams(sem, flags=None):
    return pltpu.CompilerParams(dimension_semantics=sem, vmem_limit_bytes=VMEM_LIMIT, flags=flags)


def _const_spec(shape):
    nd = len(shape)
    return pl.BlockSpec(shape, lambda *_: (0,) * nd, pipeline_mode=pl.Buffered(1))


def _rms(x, g):
    ms = jnp.mean(x * x, axis=-1, keepdims=True)
    return x * lax.rsqrt(ms + RMS_EPS) * g


def _dot(a, b):
    return jnp.dot(a, b, preferred_element_type=F32)


def _dot_nt(a, b):
    return lax.dot_general(a, b, (((1,), (1,)), ((), ())), preferred_element_type=F32)


def _ada_kernel(c_ref, w_ref, b_ref, o_ref):
    c = c_ref[...]
    ca = c * jax.nn.sigmoid(c)
    o_ref[...] = jnp.dot(ca, w_ref[...], precision=lax.Precision.HIGHEST,
                         preferred_element_type=F32) + b_ref[...]


def _ada_mod(c, ada_w, ada_b, layer):
    bsz, d = c.shape
    depth, _, n = ada_w.shape
    tn = d
    return pl.pallas_call(
        _ada_kernel,
        grid=(n // tn,),
        in_specs=[pl.BlockSpec((bsz, d), lambda j: (0, 0)),
                  pl.BlockSpec((None, d, tn), lambda j: (layer, 0, j)),
                  pl.BlockSpec((None, 1, tn), lambda j: (layer, 0, j))],
        out_specs=pl.BlockSpec((bsz, tn), lambda j: (0, j)),
        out_shape=jax.ShapeDtypeStruct((bsz, n), F32),
        compiler_params=_cparams(("arbitrary",)),
        name="ada_mod",
    )(c, ada_w, ada_b.reshape(depth, 1, n))


def _rope_kernel(pos_ref, freq_ref, cos_ref, sin_ref):
    ang = freq_ref[...] * pos_ref[0].astype(F32)
    c = jnp.cos(ang)
    s = jnp.sin(ang)
    cos_ref[0] = jnp.concatenate([c, c, c, c], axis=0).T
    sin_ref[0] = jnp.concatenate([-s, s, -s, s], axis=0).T


def _rope_tables(positions):
    bsz, seq = positions.shape
    half = A_HEAD_DIM // 2
    inv_freq = (ROPE_THETA ** (-jnp.arange(half, dtype=F32) / half)).reshape(half, 1)
    ts = 2048
    out = jax.ShapeDtypeStruct((bsz, seq, LANES), F32)
    return pl.pallas_call(
        _rope_kernel,
        grid=(bsz, seq // ts),
        in_specs=[pl.BlockSpec((1, 1, ts), lambda b, i: (b, 0, i)),
                  pl.BlockSpec((half, 1), lambda b, i: (0, 0))],
        out_specs=[pl.BlockSpec((1, ts, LANES), lambda b, i: (b, i, 0))] * 2,
        out_shape=[out, out],
        compiler_params=_cparams(("arbitrary", "arbitrary")),
        name="rope_table",
    )(positions.reshape(bsz, 1, seq), inv_freq)


def _in_proj_kernel(x_ref, mod_ref, g_ref, wm_ref, wa_ref, wif_ref, bm_ref, ba_ref, bif_ref,
                    cw_ref, cb_ref, cos_ref, sin_ref,
                    qm_ref, km_ref, vm_ref, om_ref, qa_ref, ka_ref, va_ref,
                    sga_ref, sgb_ref, zif_ref, kmean_ref, hb_ref, zbuf_ref):
    tm = x_ref.shape[1]
    it = pl.program_id(1)

    x = x_ref[0]
    sh = mod_ref[0, 0:1, :]
    sc = mod_ref[0, 1:2, :]
    hb_ref[...] = (_rms(x, g_ref[...]) * (1.0 + sc) + sh).astype(BF16)

    def proj(off, width):
        for base, w_ref, b_ref in ((OFF_IF, wif_ref, bif_ref), (OFF_AQ, wa_ref, ba_ref), (OFF_MQ, wm_ref, bm_ref)):
            if off >= base:
                lo = off - base
                return _dot_nt(hb_ref[...], w_ref[lo:lo + width, :]) + b_ref[:, lo:lo + width]

    @pl.when(it == 0)
    def _():
        zbuf_ref[0:SUBLANES, :] = jnp.zeros((SUBLANES, zbuf_ref.shape[1]), F32)

    @pl.when(it > 0)
    def _():
        zbuf_ref[0:SUBLANES, :] = zbuf_ref[tm:tm + SUBLANES, :]

    zbuf_ref[SUBLANES:SUBLANES + tm, 0:M_WIDTH] = proj(OFF_MQ, M_WIDTH)
    zbuf_ref[SUBLANES:SUBLANES + tm, M_WIDTH:2 * M_WIDTH] = proj(OFF_MK, M_WIDTH)

    def conv_silu(part):
        width = 2 * LANES
        cols = slice(part * width, (part + 1) * width)
        acc = jnp.broadcast_to(cb_ref[:, cols], (tm, width))
        for j in range(M_CONV):
            start = SUBLANES - (M_CONV - 1) + j
            acc = acc + zbuf_ref[start:start + tm, cols] * cw_ref[j:j + 1, cols]
        y = acc * jax.nn.sigmoid(acc)
        if part * width < M_WIDTH:
            qm_ref[0, :, cols] = y.astype(BF16)
        else:
            kcols = slice(part * width - M_WIDTH, (part + 1) * width - M_WIDTH)
            km_ref[0, :, kcols] = (y * M_HEAD_DIM ** -0.5).astype(BF16)

    conv_silu(0)
    vm_ref[0] = proj(OFF_MV, M_WIDTH).astype(BF16)
    conv_silu(1)
    om_ref[0] = proj(OFF_MO, M_WIDTH).astype(BF16)
    conv_silu(2)

    def rope(z):
        outs = []
        for p in range(A_WIDTH // LANES):
            zp = z[:, p * LANES:(p + 1) * LANES]
            lane = lax.broadcasted_iota(jnp.int32, zp.shape, 1)
            first_half = (lane % A_HEAD_DIM) < (A_HEAD_DIM // 2)
            rot = jnp.where(first_half,
                            pltpu.roll(zp, LANES - A_HEAD_DIM // 2, 1),
                            pltpu.roll(zp, A_HEAD_DIM // 2, 1))
            outs.append(zp * cos_ref[0] + rot * sin_ref[0])
        return outs

    q_parts = rope(proj(OFF_AQ, A_WIDTH))
    for p, qp in enumerate(q_parts):
        qa_ref[0, p * LANES:(p + 1) * LANES, :] = (qp * (LOG2_E * A_HEAD_DIM ** -0.5)).T.astype(BF16)
    conv_silu(3)

    k_parts = rope(proj(OFF_AK, A_WIDTH))
    kmean_ref[0] = jnp.zeros(kmean_ref.shape[1:], F32)
    for p, kp in enumerate(k_parts):
        ka_ref[0, :, p * LANES:(p + 1) * LANES] = kp.astype(BF16)
        for blk in range(tm // MOBA_BLOCK):
            kmean_ref[0, blk:blk + 1, p * LANES:(p + 1) * LANES] = jnp.mean(
                kp[blk * MOBA_BLOCK:(blk + 1) * MOBA_BLOCK], axis=0, keepdims=True)

    va = proj(OFF_AV, A_WIDTH)
    for p in range(A_WIDTH // LANES):
        va_ref[0, p * LANES:(p + 1) * LANES, :] = va[:, p * LANES:(p + 1) * LANES].T.astype(BF16)

    d = x.shape[1]
    for off, out_ref in ((OFF_GA, sga_ref), (OFF_GB, sgb_ref)):
        for c0 in range(0, d, 512):
            out_ref[0, :, c0:c0 + 512] = jax.nn.sigmoid(proj(off + c0, 512)).astype(BF16)

    zif_ref[0] = proj(OFF_IF, LANES)


def _in_proj(x, mod3, norm_g, weights, biases, conv_w, conv_b, cos_t, sin_t):
    bsz, seq, d = x.shape
    tm = ROW_TILE
    nt = seq // tm
    row = lambda b, i: (b, i, 0)

    def rows(width, dtype):
        return (pl.BlockSpec((1, tm, width), row), jax.ShapeDtypeStruct((bsz, seq, width), dtype))

    cols = (pl.BlockSpec((1, A_WIDTH, tm), lambda b, i: (b, 0, i)),
            jax.ShapeDtypeStruct((bsz, A_WIDTH, seq), BF16))
    outs = ([rows(M_WIDTH, BF16)] * 4 + [cols, rows(A_WIDTH, BF16), cols]
            + [rows(d, BF16)] * 2 + [rows(LANES, F32)])
    outs.append((pl.BlockSpec((1, SUBLANES, A_WIDTH), lambda b, i: (b * nt + i, 0, 0)),
                 jax.ShapeDtypeStruct((bsz * nt, SUBLANES, A_WIDTH), F32)))
    return pl.pallas_call(
        _in_proj_kernel,
        grid=(bsz, nt),
        in_specs=[pl.BlockSpec((1, tm, d), row),
                  pl.BlockSpec((1, N_MOD, d), lambda b, i: (b, 0, 0)),
                  _const_spec((1, d)),
                  *[_const_spec(t.shape) for t in (*weights, *biases)],
                  _const_spec(conv_w.shape),
                  _const_spec(conv_b.shape),
                  pl.BlockSpec((1, tm, LANES), row),
                  pl.BlockSpec((1, tm, LANES), row)],
        out_specs=[o[0] for o in outs],
        out_shape=[o[1] for o in outs],
        scratch_shapes=[pltpu.VMEM((tm, d), BF16),
                        pltpu.VMEM((tm + SUBLANES, 2 * M_WIDTH), F32)],
        compiler_params=_cparams(("arbitrary", "arbitrary")),
        name="in_proj",
    )(x, mod3, norm_g, *weights, *biases, conv_w, conv_b, cos_t, sin_t)


def _split3(x):
    hi = x.astype(BF16)
    r1 = x - hi.astype(F32)
    mid = r1.astype(BF16)
    lo = (r1 - mid.astype(F32)).astype(BF16)
    return hi, mid, lo


def _mlstm_kernel(q_ref, k_ref, v_ref, o_ref, zif_ref, g_ref, ym_ref, ct_ref, m_ref):
    bsz, L = q_ref.shape[0], q_ref.shape[1]
    dh = M_HEAD_DIM

    @pl.when(pl.program_id(0) == 0)
    def _():
        ct_ref[...] = jnp.zeros(ct_ref.shape, F32)
        m_ref[...] = jnp.zeros(m_ref.shape, F32)

    r_i = lax.broadcasted_iota(jnp.int32, (L, L), 0)
    c_i = lax.broadcasted_iota(jnp.int32, (L, L), 1)
    causal = c_i <= r_i
    triu = jnp.where(r_i <= c_i, 1.0, 0.0).astype(BF16)
    ones = jnp.ones((L, dh), BF16)
    gate_row = lax.broadcasted_iota(jnp.int32, (SUBLANES, L), 0)

    for b in range(bsz):
        gates = zif_ref[b].T[0:SUBLANES, :]
        log_f = jnp.minimum(gates, 0.0) - jnp.log1p(jnp.exp(-jnp.abs(gates)))
        log_f = jnp.where(gate_row >= M_HEADS, log_f, 0.0)
        bcum = sum(_dot(part, triu) for part in _split3(log_f))
        xrow = jnp.where(gate_row < M_HEADS, gates, bcum) * LOG2_E
        xcol = jnp.concatenate([xrow, jnp.zeros((LANES - SUBLANES, L), F32)], axis=0).T

        for h in range(M_HEADS):
            hs = slice(h * dh, (h + 1) * dh)
            st = b * M_HEADS + h
            li_col = xcol[:, h:h + 1]
            b_col = xcol[:, M_HEADS + h:M_HEADS + h + 1]
            li_row = xrow[h:h + 1, :]
            b_row = xrow[M_HEADS + h:M_HEADS + h + 1, :]
            m_prev = m_ref[st:st + 1, 0:1]
            b_last = b_col[L - 1:L, :]
            q = q_ref[b, :, hs]
            k = k_ref[b, :, hs]
            vaug = jnp.concatenate([v_ref[b, :, hs], ones], axis=1)
            ct = ct_ref[st]

            s = _dot_nt(q, k)
            dmat = jnp.where(causal, b_col + (li_row - b_row), -jnp.inf)
            inter = b_col + m_prev
            m_t = jnp.maximum(inter, jnp.max(dmat, axis=1, keepdims=True))
            sw = (s * jnp.exp2(dmat - m_t)).astype(BF16)
            nd = jnp.exp2(inter - m_t) * _dot(q, ct.astype(BF16)) + _dot(sw, vaug)
            hval = nd[:, :dh] / jnp.maximum(jnp.abs(nd[:, dh:]), jnp.exp2(-m_t))
            hn = _rms(hval, g_ref[:, hs])
            ym_ref[b, :, hs] = (hn * jax.nn.sigmoid(o_ref[b, :, hs].astype(F32))).astype(BF16)

            a_col = b_last - b_col + li_col
            m_new = jnp.maximum(b_last + m_prev, jnp.max(a_col, axis=0, keepdims=True))
            kw_t = (k.astype(F32) * jnp.exp2(a_col - m_new)).T.astype(BF16)
            ct_ref[st] = jnp.exp2(b_last + m_prev - m_new) * ct + _dot(kw_t, vaug)
            m_ref[st:st + 1, :] = jnp.broadcast_to(m_new, (1, LANES))


def _mlstm(qm, km, vm, om, zif, m_norm_g):
    bsz, seq, width = qm.shape
    L = MLSTM_CHUNK
    row = lambda c: (0, c, 0)
    return pl.pallas_call(
        _mlstm_kernel,
        grid=(seq // L,),
        in_specs=[pl.BlockSpec((bsz, L, width), row)] * 4
        + [pl.BlockSpec((bsz, L, LANES), row), _const_spec((1, width))],
        out_specs=pl.BlockSpec((bsz, L, width), row),
        out_shape=jax.ShapeDtypeStruct((bsz, seq, width), BF16),
        scratch_shapes=[pltpu.VMEM((bsz * M_HEADS, M_HEAD_DIM, 2 * M_HEAD_DIM), F32),
                        pltpu.VMEM((bsz * M_HEADS, LANES), F32)],
        compiler_params=_cparams(("arbitrary",)),
        name="mlstm",
    )(qm, km, vm, om, zif, m_norm_g)


def _moba_kernel(qt_ref, k_ref, vt_ref, kmean_ref, o_ref, qaug_ref, s_ref, p_ref, acc_ref):
    tq = blk = MOBA_BLOCK
    n_tiles = qt_ref.shape[2] // tq
    nb = k_ref.shape[1] // blk
    dh = A_HEAD_DIM
    feat = lax.broadcasted_iota(jnp.int32, (LANES, tq), 0)
    kmean = kmean_ref[0]
    blk_id = lax.broadcasted_iota(jnp.int32, (nb, tq), 0)
    lane_k = lax.broadcasted_iota(jnp.int32, (blk, LANES), 1)
    key_i = lax.broadcasted_iota(jnp.int32, (blk, 2 * tq), 0)
    qry_i = lax.broadcasted_iota(jnp.int32, (blk, 2 * tq), 1) & (tq - 1)
    ones_rows = jnp.ones((ACC_ROWS - dh, blk), BF16)

    def fold8(x, op):
        parts = [x[r:r + SUBLANES] for r in range(0, x.shape[0], SUBLANES)]
        while len(parts) > 1:
            parts = [op(a, b) for a, b in zip(parts[0::2], parts[1::2])]
        return parts[0]

    def make_tile(t):
        i = pl.program_id(2) * n_tiles + t
        qaug_t, s_t, p_t, acc_t = qaug_ref.at[t], s_ref.at[t], p_ref.at[t], acc_ref.at[t]

        def put_scores(j, slot):
            st = pl.multiple_of(j * blk, blk)
            onehot = jnp.where(lane_k == j, 1.0, 0.0).astype(BF16)
            k_aug = jnp.concatenate([k_ref[0, pl.ds(st, blk), :], onehot], axis=1)
            sj = _dot(k_aug, qaug_t[...])
            s_t[slot] = sj
            return jnp.max(fold8(sj, jnp.maximum), axis=0, keepdims=True)

        def softmax_into(s_slot, p_slot, m_new):
            rows = 4 * SUBLANES
            for r in range(0, blk, rows):
                x = (s_t[s_slot, r:r + rows, :] - m_new).astype(BF16)
                p_t[p_slot, r:r + rows, :] = jnp.exp2(x)

        def add_pv(slot, vidx, alpha):
            st = pl.multiple_of(vidx * blk, blk)
            for hh in range(2):
                qs = slice(hh * tq, (hh + 1) * tq)
                v_aug = jnp.concatenate([vt_ref[0, hh * dh:(hh + 1) * dh, pl.ds(st, blk)], ones_rows], axis=0)
                acc_t[hh] = alpha[:, qs] * acc_t[hh] + _dot(v_aug, p_t[slot, :, qs])

        def prologue():
            qt = qt_ref[0, :, t * tq:(t + 1) * tq].astype(F32)
            valid = blk_id < i
            q_own, q_aug = [], []
            for hh in range(2):
                in_head = (feat >= hh * dh) & (feat < (hh + 1) * dh)
                qh = jnp.where(in_head, qt, 0.0).astype(BF16)
                g = jnp.where(valid, _dot(kmean, qh), -jnp.inf)
                sel = jnp.zeros(g.shape, jnp.bool_)
                for _ in range(MOBA_TOPK):
                    mx = jnp.max(g, axis=0, keepdims=True)
                    idx = jnp.min(jnp.where(g == mx, blk_id, nb), axis=0, keepdims=True)
                    pick = (blk_id == idx) & valid
                    sel = sel | pick
                    g = jnp.where(pick, -jnp.inf, g)
                bias = jnp.where(sel, 0.0, NEG_BIG).astype(BF16)
                q_own.append(qh)
                q_aug.append(jnp.concatenate([qh, bias, jnp.zeros((LANES - nb, tq), BF16)], axis=0))
            qaug_t[...] = jnp.concatenate(q_aug, axis=1)

            start = pl.multiple_of(i * blk, blk)
            s = _dot(k_ref[0, pl.ds(start, blk), :], jnp.concatenate(q_own, axis=1))
            s_t[1] = jnp.where(key_i <= qry_i, s, NEG_BIG)
            m0 = jnp.max(fold8(s_t[1], jnp.maximum), axis=0, keepdims=True)
            softmax_into(1, 1, m0)
            acc_t[...] = jnp.zeros(acc_t.shape, F32)
            cmax0 = put_scores(0, 0)
            return cmax0, jnp.ones_like(m0), i, m0

        def step(j, slot, state):
            cmax, alpha_pend, vidx, m = state
            cmax_next = put_scores(jnp.minimum(j + 1, nb - 1), 1 - slot)
            add_pv(1 - slot, vidx, alpha_pend)
            m_new = jnp.maximum(m, cmax)
            softmax_into(slot, slot, m_new)
            return cmax_next, jnp.exp2(m - m_new), j, m_new

        def run(first_block, n_trips, unroll, state):
            def body(trip, st):
                for u in range(unroll):
                    st = step(first_block + unroll * trip + u, u % 2, st)
                return st
            return lax.fori_loop(0, n_trips, body, state)

        def loops(state):
            done = 0
            for unroll in MOBA_UNROLLS:
                trips = (i - done) // unroll
                state = run(done, trips, unroll, state)
                done = done + trips * unroll
            return run(done, (i - done + 1) // 2, 2, state)

        def epilogue(state):
            _, alpha_pend, vidx, _ = state
            add_pv(1, vidx, alpha_pend)
            out = jnp.concatenate([acc_t[hh, :dh, :] / acc_t[hh, dh:dh + 1, :] for hh in range(2)],
                                  axis=0)
            o_ref[0, t * tq:(t + 1) * tq, :] = out.T.astype(BF16)

        return prologue, loops, epilogue

    tiles = [make_tile(t) for t in range(n_tiles)]
    states = [prologue() for prologue, _, _ in tiles]
    states = [loops(state) for (_, loops, _), state in zip(tiles, states)]
    for (_, _, epilogue), state in zip(tiles, states):
        epilogue(state)


def _moba(qat, ka, vat, kmean):
    bsz, width, seq = qat.shape
    tq = MOBA_BLOCK
    tw = MOBA_TILES * tq
    npair = width // LANES
    return pl.pallas_call(
        _moba_kernel,
        grid=(bsz, npair, seq // tw),
        in_specs=[pl.BlockSpec((1, LANES, tw), lambda b, p, i: (b, p, i)),
                  pl.BlockSpec((1, seq, LANES), lambda b, p, i: (b, 0, p)),
                  pl.BlockSpec((1, LANES, seq), lambda b, p, i: (b, p, 0)),
                  pl.BlockSpec((1, kmean.shape[1], LANES), lambda b, p, i: (b, 0, p))],
        out_specs=pl.BlockSpec((1, tw, LANES), lambda b, p, i: (b, i, p)),
        out_shape=jax.ShapeDtypeStruct((bsz, seq, width), BF16),
        scratch_shapes=[pltpu.VMEM((MOBA_TILES, 2 * LANES, 2 * tq), BF16),
                        pltpu.VMEM((MOBA_TILES, 2, MOBA_BLOCK, 2 * tq), F32),
                        pltpu.VMEM((MOBA_TILES, 2, MOBA_BLOCK, 2 * tq), BF16),
                        pltpu.VMEM((MOBA_TILES, 2, ACC_ROWS, tq), F32)],
        compiler_params=_cparams(("arbitrary", "arbitrary", "arbitrary")),
        name="moba",
    )(qat, ka, vat, kmean)


def _tail_kernel(x_ref, ym_ref, ya_ref, sga_ref, sgb_ref, mod_ref, g2_ref, gf_ref,
                 pm_ref, pa_ref, wo_ref, wg_ref, wu_ref, wd_ref, o_ref, *, final_norm):
    gate1 = mod_ref[0, 2:3, :]
    sh = mod_ref[0, 3:4, :]
    sc = mod_ref[0, 4:5, :]
    gate = mod_ref[0, 5:6, :]

    merged = (sga_ref[0].astype(F32) * _dot(ym_ref[0], pm_ref[...])
              + sgb_ref[0].astype(F32) * _dot(ya_ref[0], pa_ref[...]))
    x = x_ref[0] + gate1 * _dot(merged.astype(BF16), wo_ref[...])

    h = (_rms(x, g2_ref[...]) * (1.0 + sc) + sh).astype(BF16)
    d_ff = wg_ref.shape[1]
    step = 256
    acc = jnp.zeros(x.shape, F32)
    for c0 in range(0, d_ff, step):
        a = _dot(h, wg_ref[:, c0:c0 + step])
        u = _dot(h, wu_ref[:, c0:c0 + step])
        f = (a * jax.nn.sigmoid(a) * u).astype(BF16)
        acc = acc + _dot(f, wd_ref[c0:c0 + step, :])
    y = x + gate * acc
    o_ref[0] = _rms(y, gf_ref[...]) if final_norm else y


def _tail(x, ym, ya, sga, sgb, mod3, norm2_g, normf_g, weights, final_norm):
    bsz, seq, d = x.shape
    tm = ROW_TILE
    row = lambda b, i: (b, i, 0)
    return pl.pallas_call(
        functools.partial(_tail_kernel, final_norm=final_norm),
        grid=(bsz, seq // tm),
        in_specs=[pl.BlockSpec((1, tm, d), row),
                  pl.BlockSpec((1, tm, ym.shape[2]), row),
                  pl.BlockSpec((1, tm, ya.shape[2]), row),
                  pl.BlockSpec((1, tm, d), row),
                  pl.BlockSpec((1, tm, d), row),
                  pl.BlockSpec((1, N_MOD, d), lambda b, i: (b, 0, 0)),
                  _const_spec((1, d)), _const_spec((1, d)),
                  *[_const_spec(w.shape) for w in weights]],
        out_specs=pl.BlockSpec((1, tm, d), row),
        out_shape=jax.ShapeDtypeStruct((bsz, seq, d), F32),
        compiler_params=_cparams(("arbitrary", "arbitrary")),
        name="merge_ffn_out",
    )(x, ym, ya, sga, sgb, mod3, norm2_g, normf_g, *weights)


def _in_proj_params(w_in, b_in, layer):
    m_end = 4 * M_WIDTH
    if_end = m_end + 2 * M_HEADS
    n_pad = LANES - 2 * M_HEADS
    wt = jnp.swapaxes(w_in[layer], 0, 1)
    b = b_in[layer].reshape(1, -1)
    weights = (wt[:m_end].astype(BF16), wt[if_end:].astype(BF16),
               jnp.pad(wt[m_end:if_end], ((0, n_pad), (0, 0))).astype(BF16))
    biases = (b[:, :m_end], b[:, if_end:], jnp.pad(b[:, m_end:if_end], ((0, 0), (0, n_pad))))
    return weights, biases


def kernel(x, c, positions, ada_w, ada_b, norm1_g, norm2_g, normf_g, w_in, b_in, conv_w, conv_b,
           m_norm_g, p_mlstm, p_moba, w_out, w_gate, w_up, w_down):
    bsz, seq, d = x.shape
    depth = ada_w.shape[0]
    n_blocks = seq // MOBA_BLOCK
    cos_t, sin_t = _rope_tables(positions)
    for l in range(depth):
        mod3 = _ada_mod(c, ada_w, ada_b, l).reshape(bsz, N_MOD, d)
        weights, biases = _in_proj_params(w_in, b_in, l)
        (qm, km, vm, om, qa, ka, va, sga, sgb, zif, kmean_tiles) = _in_proj(
            x, mod3, norm1_g[l].reshape(1, d), weights, biases, conv_w[l],
            conv_b[l].reshape(1, -1), cos_t, sin_t)
        ym = _mlstm(qm, km, vm, om, zif, m_norm_g[l].reshape(1, -1))
        kmean = kmean_tiles[:, :ROW_TILE // MOBA_BLOCK].reshape(bsz, n_blocks, A_WIDTH)
        ya = _moba(qa, ka, va, kmean.astype(BF16))
        tail_weights = [w[l].astype(BF16) for w in (p_mlstm, p_moba, w_out, w_gate, w_up, w_down)]
        x = _tail(x, ym, ya, sga, sgb, mod3, norm2_g[l].reshape(1, d), normf_g.reshape(1, d),
                  tail_weights, final_norm=(l == depth - 1))
    return x
```

```python
import functools

import jax
import jax.numpy as jnp
from jax import lax
from jax.experimental import pallas as pl
from jax.experimental.pallas import tpu as pltpu

F32 = jnp.float32
BF16 = jnp.bfloat16

M_HEADS = 4
M_HEAD_DIM = 128
M_WIDTH = M_HEADS * M_HEAD_DIM
M_CONV = 4
A_HEADS = 8
A_HEAD_DIM = 64
A_WIDTH = A_HEADS * A_HEAD_DIM
MOBA_BLOCK = 256
MOBA_TOPK = 3
ROPE_THETA = 10000.0
N_MOD = 6
RMS_EPS = 1e-6

LANES = 128
SUBLANES = 8
VMEM_LIMIT = 56 * 1024 * 1024

ROW_TILE = 512
MLSTM_CHUNK = 256
NEG_BIG = -1e30
LOG2_E = 1.4426950408889634
MOBA_TILES = 4
MOBA_UNROLLS = (8, 4)
ACC_ROWS = A_HEAD_DIM + 16

OFF_MQ, OFF_MK, OFF_MV, OFF_MO = 0, 512, 1024, 1536
OFF_AQ, OFF_AK, OFF_AV = 2048, 2560, 3072
OFF_GA, OFF_GB = 3584, 4608
OFF_IF = 5632
D_IN_PACKED = 5760


def _cparams(sem, flags=None):
    return pltpu.CompilerParams(dimension_semantics=sem, vmem_limit_bytes=VMEM_LIMIT, flags=flags)


def _const_spec(shape):
    nd = len(shape)
    return pl.BlockSpec(shape, lambda *_: (0,) * nd, pipeline_mode=pl.Buffered(1))


def _rms(x, g):
    ms = jnp.mean(x * x, axis=-1, keepdims=True)
    return x * lax.rsqrt(ms + RMS_EPS) * g


def _dot(a, b):
    return jnp.dot(a, b, preferred_element_type=F32)


def _dot_nt(a, b):
    return lax.dot_general(a, b, (((1,), (1,)), ((), ())), preferred_element_type=F32)


def _ada_kernel(c_ref, w_ref, b_ref, o_ref):
    c = c_ref[...]
    ca = c * jax.nn.sigmoid(c)
    o_ref[...] = jnp.dot(ca, w_ref[...], precision=lax.Precision.HIGHEST,
                         preferred_element_type=F32) + b_ref[...]


def _ada_mod(c, ada_w, ada_b, layer):
    bsz, d = c.shape
    depth, _, n = ada_w.shape
    tn = d
    return pl.pallas_call(
        _ada_kernel,
        grid=(n // tn,),
        in_specs=[pl.BlockSpec((bsz, d), lambda j: (0, 0)),
                  pl.BlockSpec((None, d, tn), lambda j: (layer, 0, j)),
                  pl.BlockSpec((None, 1, tn), lambda j: (layer, 0, j))],
        out_specs=pl.BlockSpec((bsz, tn), lambda j: (0, j)),
        out_shape=jax.ShapeDtypeStruct((bsz, n), F32),
        compiler_params=_cparams(("arbitrary",)),
        name="ada_mod",
    )(c, ada_w, ada_b.reshape(depth, 1, n))


def _rope_kernel(pos_ref, freq_ref, cos_ref, sin_ref):
    ang = freq_ref[...] * pos_ref[0].astype(F32)
    c = jnp.cos(ang)
    s = jnp.sin(ang)
    cos_ref[0] = jnp.concatenate([c, c, c, c], axis=0).T
    sin_ref[0] = jnp.concatenate([-s, s, -s, s], axis=0).T


def _rope_tables(positions):
    bsz, seq = positions.shape
    half = A_HEAD_DIM // 2
    inv_freq = (ROPE_THETA ** (-jnp.arange(half, dtype=F32) / half)).reshape(half, 1)
    ts = 2048
    out = jax.ShapeDtypeStruct((bsz, seq, LANES), F32)
    return pl.pallas_call(
        _rope_kernel,
        grid=(bsz, seq // ts),
        in_specs=[pl.BlockSpec((1, 1, ts), lambda b, i: (b, 0, i)),
                  pl.BlockSpec((half, 1), lambda b, i: (0, 0))],
        out_specs=[pl.BlockSpec((1, ts, LANES), lambda b, i: (b, i, 0))] * 2,
        out_shape=[out, out],
        compiler_params=_cparams(("arbitrary", "arbitrary")),
        name="rope_table",
    )(positions.reshape(bsz, 1, seq), inv_freq)


def _in_proj_kernel(x_ref, mod_ref, g_ref, wm_ref, wa_ref, wif_ref, bm_ref, ba_ref, bif_ref,
                    cw_ref, cb_ref, cos_ref, sin_ref,
                    qm_ref, km_ref, vm_ref, om_ref, qa_ref, ka_ref, va_ref,
                    sga_ref, sgb_ref, zif_ref, kmean_ref, hb_ref, zbuf_ref):
    tm = x_ref.shape[1]
    it = pl.program_id(1)

    x = x_ref[0]
    sh = mod_ref[0, 0:1, :]
    sc = mod_ref[0, 1:2, :]
    hb_ref[...] = (_rms(x, g_ref[...]) * (1.0 + sc) + sh).astype(BF16)

    def proj(off, width):
        for base, w_ref, b_ref in ((OFF_IF, wif_ref, bif_ref), (OFF_AQ, wa_ref, ba_ref), (OFF_MQ, wm_ref, bm_ref)):
            if off >= base:
                lo = off - base
                return _dot_nt(hb_ref[...], w_ref[lo:lo + width, :]) + b_ref[:, lo:lo + width]

    @pl.when(it == 0)
    def _():
        zbuf_ref[0:SUBLANES, :] = jnp.zeros((SUBLANES, zbuf_ref.shape[1]), F32)

    @pl.when(it > 0)
    def _():
        zbuf_ref[0:SUBLANES, :] = zbuf_ref[tm:tm + SUBLANES, :]

    zbuf_ref[SUBLANES:SUBLANES + tm, 0:M_WIDTH] = proj(OFF_MQ, M_WIDTH)
    zbuf_ref[SUBLANES:SUBLANES + tm, M_WIDTH:2 * M_WIDTH] = proj(OFF_MK, M_WIDTH)

    def conv_silu(part):
        width = 2 * LANES
        cols = slice(part * width, (part + 1) * width)
        acc = jnp.broadcast_to(cb_ref[:, cols], (tm, width))
        for j in range(M_CONV):
            start = SUBLANES - (M_CONV - 1) + j
            acc = acc + zbuf_ref[start:start + tm, cols] * cw_ref[j:j + 1, cols]
        y = acc * jax.nn.sigmoid(acc)
        if part * width < M_WIDTH:
            qm_ref[0, :, cols] = y.astype(BF16)
        else:
            krows = slice(part * width - M_WIDTH, (part + 1) * width - M_WIDTH)
            km_ref[0, krows, :] = (y * M_HEAD_DIM ** -0.5).T.astype(BF16)

    conv_silu(0)
    vm_ref[0] = proj(OFF_MV, M_WIDTH).astype(BF16)
    conv_silu(1)
    om_ref[0] = proj(OFF_MO, M_WIDTH).astype(BF16)
    conv_silu(2)

    def rope(z):
        outs = []
        for p in range(A_WIDTH // LANES):
            zp = z[:, p * LANES:(p + 1) * LANES]
            lane = lax.broadcasted_iota(jnp.int32, zp.shape, 1)
            first_half = (lane % A_HEAD_DIM) < (A_HEAD_DIM // 2)
            rot = jnp.where(first_half,
                            pltpu.roll(zp, LANES - A_HEAD_DIM // 2, 1),
                            pltpu.roll(zp, A_HEAD_DIM // 2, 1))
            outs.append(zp * cos_ref[0] + rot * sin_ref[0])
        return outs

    q_parts = rope(proj(OFF_AQ, A_WIDTH))
    for p, qp in enumerate(q_parts):
        qa_ref[0, p * LANES:(p + 1) * LANES, :] = (qp * (LOG2_E * A_HEAD_DIM ** -0.5)).T.astype(BF16)
    conv_silu(3)

    k_parts = rope(proj(OFF_AK, A_WIDTH))
    kmean_ref[0] = jnp.zeros(kmean_ref.shape[1:], F32)
    for p, kp in enumerate(k_parts):
        ka_ref[0, :, p * LANES:(p + 1) * LANES] = kp.astype(BF16)
        for blk in range(tm // MOBA_BLOCK):
            kmean_ref[0, blk:blk + 1, p * LANES:(p + 1) * LANES] = jnp.mean(
                kp[blk * MOBA_BLOCK:(blk + 1) * MOBA_BLOCK], axis=0, keepdims=True)

    va = proj(OFF_AV, A_WIDTH)
    for p in range(A_WIDTH // LANES):
        va_ref[0, p * LANES:(p + 1) * LANES, :] = va[:, p * LANES:(p + 1) * LANES].T.astype(BF16)

    d = x.shape[1]
    for off, out_ref in ((OFF_GA, sga_ref), (OFF_GB, sgb_ref)):
        for c0 in range(0, d, 512):
            out_ref[0, :, c0:c0 + 512] = jax.nn.sigmoid(proj(off + c0, 512)).astype(BF16)

    zif_ref[0] = proj(OFF_IF, LANES)


def _in_proj(x, mod3, norm_g, weights, biases, conv_w, conv_b, cos_t, sin_t):
    bsz, seq, d = x.shape
    tm = ROW_TILE
    nt = seq // tm
    row = lambda b, i: (b, i, 0)

    def rows(width, dtype):
        return (pl.BlockSpec((1, tm, width), row), jax.ShapeDtypeStruct((bsz, seq, width), dtype))

    cols = (pl.BlockSpec((1, A_WIDTH, tm), lambda b, i: (b, 0, i)),
            jax.ShapeDtypeStruct((bsz, A_WIDTH, seq), BF16))
    outs = ([rows(M_WIDTH, BF16), cols] + [rows(M_WIDTH, BF16)] * 2 + [cols, rows(A_WIDTH, BF16), cols]
            + [rows(d, BF16)] * 2 + [rows(LANES, F32)])
    outs.append((pl.BlockSpec((1, SUBLANES, A_WIDTH), lambda b, i: (b * nt + i, 0, 0)),
                 jax.ShapeDtypeStruct((bsz * nt, SUBLANES, A_WIDTH), F32)))
    return pl.pallas_call(
        _in_proj_kernel,
        grid=(bsz, nt),
        in_specs=[pl.BlockSpec((1, tm, d), row),
                  pl.BlockSpec((1, N_MOD, d), lambda b, i: (b, 0, 0)),
                  _const_spec((1, d)),
                  *[_const_spec(t.shape) for t in (*weights, *biases)],
                  _const_spec(conv_w.shape),
                  _const_spec(conv_b.shape),
                  pl.BlockSpec((1, tm, LANES), row),
                  pl.BlockSpec((1, tm, LANES), row)],
        out_specs=[o[0] for o in outs],
        out_shape=[o[1] for o in outs],
        scratch_shapes=[pltpu.VMEM((tm, d), BF16),
                        pltpu.VMEM((tm + SUBLANES, 2 * M_WIDTH), F32)],
        compiler_params=_cparams(("arbitrary", "arbitrary")),
        name="in_proj",
    )(x, mod3, norm_g, *weights, *biases, conv_w, conv_b, cos_t, sin_t)


def _split3(x):
    hi = x.astype(BF16)
    r1 = x - hi.astype(F32)
    mid = r1.astype(BF16)
    lo = (r1 - mid.astype(F32)).astype(BF16)
    return hi, mid, lo


def _mlstm_kernel(q_ref, kt_ref, v_ref, o_ref, zif_ref, g_ref, ym_ref, ct_ref, m_ref, s_ref, sw_ref, mt_ref):
    bsz, L = q_ref.shape[0], q_ref.shape[1]
    dh = M_HEAD_DIM

    @pl.when(pl.program_id(0) == 0)
    def _():
        ct_ref[...] = jnp.zeros(ct_ref.shape, F32)
        m_ref[...] = jnp.zeros(m_ref.shape, F32)

    r_i = lax.broadcasted_iota(jnp.int32, (L, L), 0)
    c_i = lax.broadcasted_iota(jnp.int32, (L, L), 1)
    causal = c_i <= r_i
    triu = jnp.where(r_i <= c_i, 1.0, 0.0).astype(BF16)
    ones = jnp.ones((L, dh), BF16)
    gate_row = lax.broadcasted_iota(jnp.int32, (SUBLANES, L), 0)

    gate_logs = []
    for b in range(bsz):
        gates = zif_ref[b].T[0:SUBLANES, :]
        log_f = jnp.minimum(gates, 0.0) - jnp.log1p(jnp.exp(-jnp.abs(gates)))
        log_f = jnp.where(gate_row >= M_HEADS, log_f, 0.0)
        bcum = sum(_dot(part, triu) for part in _split3(log_f))
        xrow = jnp.where(gate_row < M_HEADS, gates, bcum) * LOG2_E
        xcol = jnp.concatenate([xrow, jnp.zeros((LANES - SUBLANES, L), F32)], axis=0).T
        gate_logs.append((xrow, xcol))

    chains = [(b, h) for b in range(bsz) for h in range(M_HEADS)]

    def gate_terms(b, h):
        xrow, xcol = gate_logs[b]
        li_col = xcol[:, h:h + 1]
        b_col = xcol[:, M_HEADS + h:M_HEADS + h + 1]
        li_row = xrow[h:h + 1, :]
        b_row = xrow[M_HEADS + h:M_HEADS + h + 1, :]
        return li_col, b_col, li_row, b_row, b_col[L - 1:L, :]

    def v_aug(b, h):
        return jnp.concatenate([v_ref[b, :, h * dh:(h + 1) * dh], ones], axis=1)

    for st, (b, h) in enumerate(chains):
        hs = slice(h * dh, (h + 1) * dh)
        s_ref[st] = _dot(q_ref[b, :, hs], kt_ref[b, hs, :])

    for st, (b, h) in enumerate(chains):
        _, b_col, li_row, b_row, _ = gate_terms(b, h)
        dmat = jnp.where(causal, b_col + (li_row - b_row), -jnp.inf)
        inter = b_col + m_ref[st:st + 1, 0:1]
        m_t = jnp.maximum(inter, jnp.max(dmat, axis=1, keepdims=True))
        sw_ref[st] = (s_ref[st] * jnp.exp2(dmat - m_t)).astype(BF16)
        mt_ref[st, 0] = jnp.broadcast_to(m_t, (L, LANES))
        mt_ref[st, 1] = jnp.broadcast_to(inter - m_t, (L, LANES))

    for st, (b, h) in enumerate(chains):
        hs = slice(h * dh, (h + 1) * dh)
        m_t = mt_ref[st, 0]
        inter_w = jnp.exp2(mt_ref[st, 1])
        qc = _dot(q_ref[b, :, hs], ct_ref[st].astype(BF16))
        nd = jnp.concatenate([inter_w, inter_w], axis=1) * qc + _dot(sw_ref[st], v_aug(b, h))
        hval = nd[:, :dh] / jnp.maximum(jnp.abs(nd[:, dh:]), jnp.exp2(-m_t))
        hn = _rms(hval, g_ref[:, hs])
        ym_ref[b, :, hs] = (hn * jax.nn.sigmoid(o_ref[b, :, hs].astype(F32))).astype(BF16)

    for st, (b, h) in enumerate(chains):
        hs = slice(h * dh, (h + 1) * dh)
        _, _, li_row, b_row, b_last = gate_terms(b, h)
        m_prev = m_ref[st:st + 1, 0:1]
        a_row = b_last - b_row + li_row
        m_new = jnp.maximum(b_last + m_prev, jnp.max(a_row, axis=1, keepdims=True))
        kw_t = (kt_ref[b, hs, :].astype(F32) * jnp.exp2(a_row - m_new)).astype(BF16)
        ct_ref[st] = jnp.exp2(b_last + m_prev - m_new) * ct_ref[st] + _dot(kw_t, v_aug(b, h))
        m_ref[st:st + 1, :] = jnp.broadcast_to(m_new, (1, LANES))


def _mlstm(qm, kmt, vm, om, zif, m_norm_g):
    bsz, seq, width = qm.shape
    L = MLSTM_CHUNK
    row = lambda c: (0, c, 0)
    rows = pl.BlockSpec((bsz, L, width), row)
    return pl.pallas_call(
        _mlstm_kernel,
        grid=(seq // L,),
        in_specs=[rows, pl.BlockSpec((bsz, width, L), lambda c: (0, 0, c)), rows, rows,
                  pl.BlockSpec((bsz, L, LANES), row), _const_spec((1, width))],
        out_specs=pl.BlockSpec((bsz, L, width), row),
        out_shape=jax.ShapeDtypeStruct((bsz, seq, width), BF16),
        scratch_shapes=[pltpu.VMEM((bsz * M_HEADS, M_HEAD_DIM, 2 * M_HEAD_DIM), F32),
                        pltpu.VMEM((bsz * M_HEADS, LANES), F32),
                        pltpu.VMEM((bsz * M_HEADS, L, L), F32),
                        pltpu.VMEM((bsz * M_HEADS, L, L), BF16),
                        pltpu.VMEM((bsz * M_HEADS, 2, L, LANES), F32)],
        compiler_params=_cparams(("arbitrary",)),
        name="mlstm",
    )(qm, kmt, vm, om, zif, m_norm_g)


def _moba_kernel(qt_ref, k_ref, vt_ref, kmean_ref, o_ref, qaug_ref, s_ref, p_ref, acc_ref):
    tq = blk = MOBA_BLOCK
    n_tiles = qt_ref.shape[2] // tq
    nb = k_ref.shape[1] // blk
    dh = A_HEAD_DIM
    feat = lax.broadcasted_iota(jnp.int32, (LANES, tq), 0)
    kmean = kmean_ref[0]
    blk_id = lax.broadcasted_iota(jnp.int32, (nb, tq), 0)
    lane_k = lax.broadcasted_iota(jnp.int32, (blk, LANES), 1)
    key_i = lax.broadcasted_iota(jnp.int32, (blk, 2 * tq), 0)
    qry_i = lax.broadcasted_iota(jnp.int32, (blk, 2 * tq), 1) & (tq - 1)
    ones_rows = jnp.ones((ACC_ROWS - dh, blk), BF16)

    def fold8(x, op):
        parts = [x[r:r + SUBLANES] for r in range(0, x.shape[0], SUBLANES)]
        while len(parts) > 1:
            parts = [op(a, b) for a, b in zip(parts[0::2], parts[1::2])]
        return parts[0]

    def make_tile(t):
        i = pl.program_id(2) * n_tiles + t
        qaug_t, s_t, p_t, acc_t = qaug_ref.at[t], s_ref.at[t], p_ref.at[t], acc_ref.at[t]

        def put_scores(j, slot):
            st = pl.multiple_of(j * blk, blk)
            onehot = jnp.where(lane_k == j, 1.0, 0.0).astype(BF16)
            k_aug = jnp.concatenate([k_ref[0, pl.ds(st, blk), :], onehot], axis=1)
            sj = _dot(k_aug, qaug_t[...])
            s_t[slot] = sj
            return jnp.max(fold8(sj, jnp.maximum), axis=0, keepdims=True)

        def softmax_into(s_slot, p_slot, m_new):
            rows = 4 * SUBLANES
            for r in range(0, blk, rows):
                x = (s_t[s_slot, r:r + rows, :] - m_new).astype(BF16)
                p_t[p_slot, r:r + rows, :] = jnp.exp2(x)

        def add_pv(slot, vidx, alpha):
            st = pl.multiple_of(vidx * blk, blk)
            for hh in range(2):
                qs = slice(hh * tq, (hh + 1) * tq)
                v_aug = jnp.concatenate([vt_ref[0, hh * dh:(hh + 1) * dh, pl.ds(st, blk)], ones_rows], axis=0)
                acc_t[hh] = alpha[:, qs] * acc_t[hh] + _dot(v_aug, p_t[slot, :, qs])

        def prologue():
            qt = qt_ref[0, :, t * tq:(t + 1) * tq].astype(F32)
            valid = blk_id < i
            q_own, q_aug = [], []
            for hh in range(2):
                in_head = (feat >= hh * dh) & (feat < (hh + 1) * dh)
                qh = jnp.where(in_head, qt, 0.0).astype(BF16)
                g = jnp.where(valid, _dot(kmean, qh), -jnp.inf)
                sel = jnp.zeros(g.shape, jnp.bool_)
                for _ in range(MOBA_TOPK):
                    mx = jnp.max(g, axis=0, keepdims=True)
                    idx = jnp.min(jnp.where(g == mx, blk_id, nb), axis=0, keepdims=True)
                    pick = (blk_id == idx) & valid
                    sel = sel | pick
                    g = jnp.where(pick, -jnp.inf, g)
                bias = jnp.where(sel, 0.0, NEG_BIG).astype(BF16)
                q_own.append(qh)
                q_aug.append(jnp.concatenate([qh, bias, jnp.zeros((LANES - nb, tq), BF16)], axis=0))
            qaug_t[...] = jnp.concatenate(q_aug, axis=1)

            start = pl.multiple_of(i * blk, blk)
            s = _dot(k_ref[0, pl.ds(start, blk), :], jnp.concatenate(q_own, axis=1))
            s_t[1] = jnp.where(key_i <= qry_i, s, NEG_BIG)
            m0 = jnp.max(fold8(s_t[1], jnp.maximum), axis=0, keepdims=True)
            softmax_into(1, 1, m0)
            acc_t[...] = jnp.zeros(acc_t.shape, F32)
            cmax0 = put_scores(0, 0)
            return cmax0, jnp.ones_like(m0), i, m0

        def step(j, slot, state):
            cmax, alpha_pend, vidx, m = state
            cmax_next = put_scores(jnp.minimum(j + 1, nb - 1), 1 - slot)
            add_pv(1 - slot, vidx, alpha_pend)
            m_new = jnp.maximum(m, cmax)
            softmax_into(slot, slot, m_new)
            return cmax_next, jnp.exp2(m - m_new), j, m_new

        def run(first_block, n_trips, unroll, state):
            def body(trip, st):
                for u in range(unroll):
                    st = step(first_block + unroll * trip + u, u % 2, st)
                return st
            return lax.fori_loop(0, n_trips, body, state)

        def loops(state):
            done = 0
            for unroll in MOBA_UNROLLS:
                trips = (i - done) // unroll
                state = run(done, trips, unroll, state)
                done = done + trips * unroll
            return run(done, (i - done + 1) // 2, 2, state)

        def epilogue(state):
            _, alpha_pend, vidx, _ = state
            add_pv(1, vidx, alpha_pend)
            out = jnp.concatenate([acc_t[hh, :dh, :] / acc_t[hh, dh:dh + 1, :] for hh in range(2)],
                                  axis=0)
            o_ref[0, t * tq:(t + 1) * tq, :] = out.T.astype(BF16)

        return prologue, loops, epilogue

    tiles = [make_tile(t) for t in range(n_tiles)]
    states = [prologue() for prologue, _, _ in tiles]
    states = [loops(state) for (_, loops, _), state in zip(tiles, states)]
    for (_, _, epilogue), state in zip(tiles, states):
        epilogue(state)


def _moba(qat, ka, vat, kmean):
    bsz, width, seq = qat.shape
    tq = MOBA_BLOCK
    tw = MOBA_TILES * tq
    npair = width // LANES
    return pl.pallas_call(
        _moba_kernel,
        grid=(bsz, npair, seq // tw),
        in_specs=[pl.BlockSpec((1, LANES, tw), lambda b, p, i: (b, p, i)),
                  pl.BlockSpec((1, seq, LANES), lambda b, p, i: (b, 0, p)),
                  pl.BlockSpec((1, LANES, seq), lambda b, p, i: (b, p, 0)),
                  pl.BlockSpec((1, kmean.shape[1], LANES), lambda b, p, i: (b, 0, p))],
        out_specs=pl.BlockSpec((1, tw, LANES), lambda b, p, i: (b, i, p)),
        out_shape=jax.ShapeDtypeStruct((bsz, seq, width), BF16),
        scratch_shapes=[pltpu.VMEM((MOBA_TILES, 2 * LANES, 2 * tq), BF16),
                        pltpu.VMEM((MOBA_TILES, 2, MOBA_BLOCK, 2 * tq), F32),
                        pltpu.VMEM((MOBA_TILES, 2, MOBA_BLOCK, 2 * tq), BF16),
                        pltpu.VMEM((MOBA_TILES, 2, ACC_ROWS, tq), F32)],
        compiler_params=_cparams(("arbitrary", "arbitrary", "arbitrary")),
        name="moba",
    )(qat, ka, vat, kmean)


def _tail_kernel(x_ref, ym_ref, ya_ref, sga_ref, sgb_ref, mod_ref, g2_ref, gf_ref,
                 pm_ref, pa_ref, wo_ref, wg_ref, wu_ref, wd_ref, o_ref, *, final_norm):
    gate1 = mod_ref[0, 2:3, :]
    sh = mod_ref[0, 3:4, :]
    sc = mod_ref[0, 4:5, :]
    gate = mod_ref[0, 5:6, :]

    merged = (sga_ref[0].astype(F32) * _dot(ym_ref[0], pm_ref[...])
              + sgb_ref[0].astype(F32) * _dot(ya_ref[0], pa_ref[...]))
    x = x_ref[0] + gate1 * _dot(merged.astype(BF16), wo_ref[...])

    h = (_rms(x, g2_ref[...]) * (1.0 + sc) + sh).astype(BF16)
    d_ff = wg_ref.shape[1]
    step = 256
    acc = jnp.zeros(x.shape, F32)
    for c0 in range(0, d_ff, step):
        a = _dot(h, wg_ref[:, c0:c0 + step])
        u = _dot(h, wu_ref[:, c0:c0 + step])
        f = (a * jax.nn.sigmoid(a) * u).astype(BF16)
        acc = acc + _dot(f, wd_ref[c0:c0 + step, :])
    y = x + gate * acc
    o_ref[0] = _rms(y, gf_ref[...]) if final_norm else y


def _tail(x, ym, ya, sga, sgb, mod3, norm2_g, normf_g, weights, final_norm):
    bsz, seq, d = x.shape
    tm = ROW_TILE
    row = lambda b, i: (b, i, 0)
    return pl.pallas_call(
        functools.partial(_tail_kernel, final_norm=final_norm),
        grid=(bsz, seq // tm),
        in_specs=[pl.BlockSpec((1, tm, d), row),
                  pl.BlockSpec((1, tm, ym.shape[2]), row),
                  pl.BlockSpec((1, tm, ya.shape[2]), row),
                  pl.BlockSpec((1, tm, d), row),
                  pl.BlockSpec((1, tm, d), row),
                  pl.BlockSpec((1, N_MOD, d), lambda b, i: (b, 0, 0)),
                  _const_spec((1, d)), _const_spec((1, d)),
                  *[_const_spec(w.shape) for w in weights]],
        out_specs=pl.BlockSpec((1, tm, d), row),
        out_shape=jax.ShapeDtypeStruct((bsz, seq, d), F32),
        compiler_params=_cparams(("arbitrary", "arbitrary")),
        name="merge_ffn_out",
    )(x, ym, ya, sga, sgb, mod3, norm2_g, normf_g, *weights)


def _in_proj_params(w_in, b_in, layer):
    m_end = 4 * M_WIDTH
    if_end = m_end + 2 * M_HEADS
    n_pad = LANES - 2 * M_HEADS
    wt = jnp.swapaxes(w_in[layer], 0, 1)
    b = b_in[layer].reshape(1, -1)
    weights = (wt[:m_end].astype(BF16), wt[if_end:].astype(BF16),
               jnp.pad(wt[m_end:if_end], ((0, n_pad), (0, 0))).astype(BF16))
    biases = (b[:, :m_end], b[:, if_end:], jnp.pad(b[:, m_end:if_end], ((0, 0), (0, n_pad))))
    return weights, biases


def kernel(x, c, positions, ada_w, ada_b, norm1_g, norm2_g, normf_g, w_in, b_in, conv_w, conv_b,
           m_norm_g, p_mlstm, p_moba, w_out, w_gate, w_up, w_down):
    bsz, seq, d = x.shape
    depth = ada_w.shape[0]
    n_blocks = seq // MOBA_BLOCK
    cos_t, sin_t = _rope_tables(positions)
    for l in range(depth):
        mod3 = _ada_mod(c, ada_w, ada_b, l).reshape(bsz, N_MOD, d)
        weights, biases = _in_proj_params(w_in, b_in, l)
        (qm, km, vm, om, qa, ka, va, sga, sgb, zif, kmean_tiles) = _in_proj(
            x, mod3, norm1_g[l].reshape(1, d), weights, biases, conv_w[l],
            conv_b[l].reshape(1, -1), cos_t, sin_t)
        ym = _mlstm(qm, km, vm, om, zif, m_norm_g[l].reshape(1, -1))
        kmean = kmean_tiles[:, :ROW_TILE // MOBA_BLOCK].reshape(bsz, n_blocks, A_WIDTH)
        ya = _moba(qa, ka, va, kmean.astype(BF16))
        tail_weights = [w[l].astype(BF16) for w in (p_mlstm, p_moba, w_out, w_gate, w_up, w_down)]
        x = _tail(x, ym, ya, sga, sgb, mod3, norm2_g[l].reshape(1, d), normf_g.reshape(1, d),
                  tail_weights, final_norm=(l == depth - 1))
    return x
```

```python
import functools

import jax
import jax.numpy as jnp
from jax import lax
from jax.experimental import pallas as pl
from jax.experimental.pallas import tpu as pltpu

F32 = jnp.float32
BF16 = jnp.bfloat16

M_HEADS = 4
M_HEAD_DIM = 128
M_WIDTH = M_HEADS * M_HEAD_DIM
M_CONV = 4
A_HEADS = 8
A_HEAD_DIM = 64
A_WIDTH = A_HEADS * A_HEAD_DIM
MOBA_BLOCK = 256
MOBA_TOPK = 3
ROPE_THETA = 10000.0
N_MOD = 6
RMS_EPS = 1e-6

LANES = 128
SUBLANES = 8
VMEM_LIMIT = 56 * 1024 * 1024

ROW_TILE = 512
IN_ROW_TILE = 512
MLSTM_CHUNK = 256
NEG_BIG = -1e30
LOG2_E = 1.4426950408889634
MOBA_TILES = 8
MOBA_UNROLLS = (8, 4)
ACC_ROWS = A_HEAD_DIM + 16

OFF_MQ, OFF_MK, OFF_MV, OFF_MO = 0, 512, 1024, 1536
OFF_AQ, OFF_AK, OFF_AV = 2048, 2560, 3072
OFF_GA, OFF_GB = 3584, 4608
OFF_IF = 5632
D_IN_PACKED = 5760


def _cparams(sem, flags=None):
    return pltpu.CompilerParams(dimension_semantics=sem, vmem_limit_bytes=VMEM_LIMIT, flags=flags)


def _const_spec(shape):
    nd = len(shape)
    return pl.BlockSpec(shape, lambda *_: (0,) * nd, pipeline_mode=pl.Buffered(1))


def _rms(x, g):
    ms = jnp.mean(x * x, axis=-1, keepdims=True)
    return x * lax.rsqrt(ms + RMS_EPS) * g


def _dot(a, b):
    return jnp.dot(a, b, preferred_element_type=F32)


def _dot_nt(a, b):
    return lax.dot_general(a, b, (((1,), (1,)), ((), ())), preferred_element_type=F32)


def _ada_kernel(c_ref, w_ref, b_ref, o_ref):
    c = c_ref[...]
    ca = c * jax.nn.sigmoid(c)
    o_ref[...] = jnp.dot(ca, w_ref[...], precision=lax.Precision.HIGHEST,
                         preferred_element_type=F32) + b_ref[...]


def _ada_mod(c, ada_w, ada_b, layer):
    bsz, d = c.shape
    depth, _, n = ada_w.shape
    tn = d
    return pl.pallas_call(
        _ada_kernel,
        grid=(n // tn,),
        in_specs=[pl.BlockSpec((bsz, d), lambda j: (0, 0)),
                  pl.BlockSpec((None, d, tn), lambda j: (layer, 0, j)),
                  pl.BlockSpec((None, 1, tn), lambda j: (layer, 0, j))],
        out_specs=pl.BlockSpec((bsz, tn), lambda j: (0, j)),
        out_shape=jax.ShapeDtypeStruct((bsz, n), F32),
        compiler_params=_cparams(("arbitrary",)),
        name="ada_mod",
    )(c, ada_w, ada_b.reshape(depth, 1, n))


def _rope_kernel(pos_ref, freq_ref, cos_ref, sin_ref):
    ang = freq_ref[...] * pos_ref[0].astype(F32)
    c = jnp.cos(ang)
    s = jnp.sin(ang)
    cos_ref[0] = jnp.concatenate([c, c, c, c], axis=0).T
    sin_ref[0] = jnp.concatenate([-s, s, -s, s], axis=0).T


def _rope_tables(positions):
    bsz, seq = positions.shape
    half = A_HEAD_DIM // 2
    inv_freq = (ROPE_THETA ** (-jnp.arange(half, dtype=F32) / half)).reshape(half, 1)
    ts = 2048
    out = jax.ShapeDtypeStruct((bsz, seq, LANES), F32)
    return pl.pallas_call(
        _rope_kernel,
        grid=(bsz, seq // ts),
        in_specs=[pl.BlockSpec((1, 1, ts), lambda b, i: (b, 0, i)),
                  pl.BlockSpec((half, 1), lambda b, i: (0, 0))],
        out_specs=[pl.BlockSpec((1, ts, LANES), lambda b, i: (b, i, 0))] * 2,
        out_shape=[out, out],
        compiler_params=_cparams(("arbitrary", "arbitrary")),
        name="rope_table",
    )(positions.reshape(bsz, 1, seq), inv_freq)


def _in_proj_kernel(x_ref, mod_ref, g_ref, wm_ref, wa_ref, wif_ref, bm_ref, ba_ref, bif_ref,
                    cw_ref, cb_ref, cos_ref, sin_ref,
                    qm_ref, km_ref, vm_ref, om_ref, qa_ref, ka_ref, va_ref,
                    sga_ref, sgb_ref, zif_ref, kmean_ref, hb_ref, zbuf_ref):
    tm = x_ref.shape[1]
    it = pl.program_id(1)

    x = x_ref[0]
    sh = mod_ref[0, 0:1, :]
    sc = mod_ref[0, 1:2, :]
    hb_ref[...] = (_rms(x, g_ref[...]) * (1.0 + sc) + sh).astype(BF16)

    def proj(off, width):
        for base, w_ref, b_ref in ((OFF_IF, wif_ref, bif_ref), (OFF_AQ, wa_ref, ba_ref), (OFF_MQ, wm_ref, bm_ref)):
            if off >= base:
                lo = off - base
                return _dot_nt(hb_ref[...], w_ref[lo:lo + width, :]) + b_ref[:, lo:lo + width]

    @pl.when(it == 0)
    def _():
        zbuf_ref[0:SUBLANES, :] = jnp.zeros((SUBLANES, zbuf_ref.shape[1]), F32)

    @pl.when(it > 0)
    def _():
        zbuf_ref[0:SUBLANES, :] = zbuf_ref[tm:tm + SUBLANES, :]

    zbuf_ref[SUBLANES:SUBLANES + tm, 0:M_WIDTH] = proj(OFF_MQ, M_WIDTH)
    zbuf_ref[SUBLANES:SUBLANES + tm, M_WIDTH:2 * M_WIDTH] = proj(OFF_MK, M_WIDTH)

    def conv_silu(part):
        width = 2 * LANES
        cols = slice(part * width, (part + 1) * width)
        acc = jnp.broadcast_to(cb_ref[:, cols], (tm, width))
        for j in range(M_CONV):
            start = SUBLANES - (M_CONV - 1) + j
            acc = acc + zbuf_ref[start:start + tm, cols] * cw_ref[j:j + 1, cols]
        y = acc * jax.nn.sigmoid(acc)
        if part * width < M_WIDTH:
            qm_ref[0, :, cols] = y.astype(BF16)
        else:
            krows = slice(part * width - M_WIDTH, (part + 1) * width - M_WIDTH)
            km_ref[0, krows, :] = (y * M_HEAD_DIM ** -0.5).T.astype(BF16)

    conv_silu(0)
    vm_ref[0] = proj(OFF_MV, M_WIDTH).astype(BF16)
    conv_silu(1)
    om_ref[0] = proj(OFF_MO, M_WIDTH).astype(BF16)
    conv_silu(2)

    def rope(z):
        outs = []
        for p in range(A_WIDTH // LANES):
            zp = z[:, p * LANES:(p + 1) * LANES]
            lane = lax.broadcasted_iota(jnp.int32, zp.shape, 1)
            first_half = (lane % A_HEAD_DIM) < (A_HEAD_DIM // 2)
            rot = jnp.where(first_half,
                            pltpu.roll(zp, LANES - A_HEAD_DIM // 2, 1),
                            pltpu.roll(zp, A_HEAD_DIM // 2, 1))
            outs.append(zp * cos_ref[0] + rot * sin_ref[0])
        return outs

    q_parts = rope(proj(OFF_AQ, A_WIDTH))
    for p, qp in enumerate(q_parts):
        qa_ref[0, p * LANES:(p + 1) * LANES, :] = (qp * (LOG2_E * A_HEAD_DIM ** -0.5)).T.astype(BF16)
    conv_silu(3)

    k_parts = rope(proj(OFF_AK, A_WIDTH))
    kmean_ref[0] = jnp.zeros(kmean_ref.shape[1:], F32)
    for p, kp in enumerate(k_parts):
        ka_ref[0, :, p * LANES:(p + 1) * LANES] = kp.astype(BF16)
        for blk in range(tm // MOBA_BLOCK):
            kmean_ref[0, blk:blk + 1, p * LANES:(p + 1) * LANES] = jnp.mean(
                kp[blk * MOBA_BLOCK:(blk + 1) * MOBA_BLOCK], axis=0, keepdims=True)

    va = proj(OFF_AV, A_WIDTH)
    for p in range(A_WIDTH // LANES):
        va_ref[0, p * LANES:(p + 1) * LANES, :] = va[:, p * LANES:(p + 1) * LANES].T.astype(BF16)

    d = x.shape[1]
    for off, out_ref in ((OFF_GA, sga_ref), (OFF_GB, sgb_ref)):
        for c0 in range(0, d, 512):
            out_ref[0, :, c0:c0 + 512] = jax.nn.sigmoid(proj(off + c0, 512)).astype(BF16)

    zif_ref[0] = proj(OFF_IF, LANES)


def _in_proj(x, mod3, norm_g, weights, biases, conv_w, conv_b, cos_t, sin_t):
    bsz, seq, d = x.shape
    tm = IN_ROW_TILE
    nt = seq // tm
    row = lambda b, i: (b, i, 0)

    def rows(width, dtype):
        return (pl.BlockSpec((1, tm, width), row), jax.ShapeDtypeStruct((bsz, seq, width), dtype))

    cols = (pl.BlockSpec((1, A_WIDTH, tm), lambda b, i: (b, 0, i)),
            jax.ShapeDtypeStruct((bsz, A_WIDTH, seq), BF16))
    outs = ([rows(M_WIDTH, BF16), cols] + [rows(M_WIDTH, BF16)] * 2 + [cols, rows(A_WIDTH, BF16), cols]
            + [rows(d, BF16)] * 2 + [rows(LANES, F32)])
    outs.append((pl.BlockSpec((1, SUBLANES, A_WIDTH), lambda b, i: (b * nt + i, 0, 0)),
                 jax.ShapeDtypeStruct((bsz * nt, SUBLANES, A_WIDTH), F32)))
    return pl.pallas_call(
        _in_proj_kernel,
        grid=(bsz, nt),
        in_specs=[pl.BlockSpec((1, tm, d), row),
                  pl.BlockSpec((1, N_MOD, d), lambda b, i: (b, 0, 0)),
                  _const_spec((1, d)),
                  *[_const_spec(t.shape) for t in (*weights, *biases)],
                  _const_spec(conv_w.shape),
                  _const_spec(conv_b.shape),
                  pl.BlockSpec((1, tm, LANES), row),
                  pl.BlockSpec((1, tm, LANES), row)],
        out_specs=[o[0] for o in outs],
        out_shape=[o[1] for o in outs],
        scratch_shapes=[pltpu.VMEM((tm, d), BF16),
                        pltpu.VMEM((tm + SUBLANES, 2 * M_WIDTH), F32)],
        compiler_params=_cparams(("arbitrary", "arbitrary")),
        name="in_proj",
    )(x, mod3, norm_g, *weights, *biases, conv_w, conv_b, cos_t, sin_t)


def _split3(x):
    hi = x.astype(BF16)
    r1 = x - hi.astype(F32)
    mid = r1.astype(BF16)
    lo = (r1 - mid.astype(F32)).astype(BF16)
    return hi, mid, lo


def _mlstm_kernel(q_ref, kt_ref, v_ref, o_ref, zif_ref, g_ref, ym_ref, ct_ref, m_ref, s_ref, sw_ref, mt_ref):
    bsz, L = q_ref.shape[0], q_ref.shape[1]
    dh = M_HEAD_DIM

    @pl.when(pl.program_id(0) == 0)
    def _():
        ct_ref[...] = jnp.zeros(ct_ref.shape, F32)
        m_ref[...] = jnp.zeros(m_ref.shape, F32)

    r_i = lax.broadcasted_iota(jnp.int32, (L, L), 0)
    c_i = lax.broadcasted_iota(jnp.int32, (L, L), 1)
    causal = c_i <= r_i
    triu = jnp.where(r_i <= c_i, 1.0, 0.0).astype(BF16)
    ones = jnp.ones((L, dh), BF16)
    gate_row = lax.broadcasted_iota(jnp.int32, (SUBLANES, L), 0)

    gate_logs = []
    for b in range(bsz):
        gates = zif_ref[b].T[0:SUBLANES, :]
        log_f = jnp.minimum(gates, 0.0) - jnp.log1p(jnp.exp(-jnp.abs(gates)))
        log_f = jnp.where(gate_row >= M_HEADS, log_f, 0.0)
        bcum = sum(_dot(part, triu) for part in _split3(log_f))
        xrow = jnp.where(gate_row < M_HEADS, gates, bcum) * LOG2_E
        xcol = jnp.concatenate([xrow, jnp.zeros((LANES - SUBLANES, L), F32)], axis=0).T
        gate_logs.append((xrow, xcol))

    chains = [(b, h) for b in range(bsz) for h in range(M_HEADS)]

    def gate_terms(b, h):
        xrow, xcol = gate_logs[b]
        li_col = xcol[:, h:h + 1]
        b_col = xcol[:, M_HEADS + h:M_HEADS + h + 1]
        li_row = xrow[h:h + 1, :]
        b_row = xrow[M_HEADS + h:M_HEADS + h + 1, :]
        return li_col, b_col, li_row, b_row, b_col[L - 1:L, :]

    def v_aug(b, h):
        return jnp.concatenate([v_ref[b, :, h * dh:(h + 1) * dh], ones], axis=1)

    for st, (b, h) in enumerate(chains):
        hs = slice(h * dh, (h + 1) * dh)
        s_ref[st] = _dot(q_ref[b, :, hs], kt_ref[b, hs, :])

    for st, (b, h) in enumerate(chains):
        _, b_col, li_row, b_row, _ = gate_terms(b, h)
        dmat = jnp.where(causal, b_col + (li_row - b_row), -jnp.inf)
        inter = b_col + m_ref[st:st + 1, 0:1]
        m_t = jnp.maximum(inter, jnp.max(dmat, axis=1, keepdims=True))
        sw_ref[st] = (s_ref[st] * jnp.exp2(dmat - m_t)).astype(BF16)
        mt_ref[st, 0] = jnp.broadcast_to(m_t, (L, LANES))
        mt_ref[st, 1] = jnp.broadcast_to(inter - m_t, (L, LANES))

    for st, (b, h) in enumerate(chains):
        hs = slice(h * dh, (h + 1) * dh)
        m_t = mt_ref[st, 0]
        inter_w = jnp.exp2(mt_ref[st, 1])
        qc = _dot(q_ref[b, :, hs], ct_ref[st].astype(BF16))
        nd = jnp.concatenate([inter_w, inter_w], axis=1) * qc + _dot(sw_ref[st], v_aug(b, h))
        hval = nd[:, :dh] / jnp.maximum(jnp.abs(nd[:, dh:]), jnp.exp2(-m_t))
        hn = _rms(hval, g_ref[:, hs])
        ym_ref[b, :, hs] = (hn * jax.nn.sigmoid(o_ref[b, :, hs].astype(F32))).astype(BF16)

    for st, (b, h) in enumerate(chains):
        hs = slice(h * dh, (h + 1) * dh)
        _, _, li_row, b_row, b_last = gate_terms(b, h)
        m_prev = m_ref[st:st + 1, 0:1]
        a_row = b_last - b_row + li_row
        m_new = jnp.maximum(b_last + m_prev, jnp.max(a_row, axis=1, keepdims=True))
        kw_t = (kt_ref[b, hs, :].astype(F32) * jnp.exp2(a_row - m_new)).astype(BF16)
        ct_ref[st] = jnp.exp2(b_last + m_prev - m_new) * ct_ref[st] + _dot(kw_t, v_aug(b, h))
        m_ref[st:st + 1, :] = jnp.broadcast_to(m_new, (1, LANES))


def _mlstm(qm, kmt, vm, om, zif, m_norm_g):
    bsz, seq, width = qm.shape
    L = MLSTM_CHUNK
    row = lambda c: (0, c, 0)
    rows = pl.BlockSpec((bsz, L, width), row)
    return pl.pallas_call(
        _mlstm_kernel,
        grid=(seq // L,),
        in_specs=[rows, pl.BlockSpec((bsz, width, L), lambda c: (0, 0, c)), rows, rows,
                  pl.BlockSpec((bsz, L, LANES), row), _const_spec((1, width))],
        out_specs=pl.BlockSpec((bsz, L, width), row),
        out_shape=jax.ShapeDtypeStruct((bsz, seq, width), BF16),
        scratch_shapes=[pltpu.VMEM((bsz * M_HEADS, M_HEAD_DIM, 2 * M_HEAD_DIM), F32),
                        pltpu.VMEM((bsz * M_HEADS, LANES), F32),
                        pltpu.VMEM((bsz * M_HEADS, L, L), F32),
                        pltpu.VMEM((bsz * M_HEADS, L, L), BF16),
                        pltpu.VMEM((bsz * M_HEADS, 2, L, LANES), F32)],
        compiler_params=_cparams(("arbitrary",)),
        name="mlstm",
    )(qm, kmt, vm, om, zif, m_norm_g)


def _moba_kernel(qt_ref, k_ref, vt_ref, kmean_ref, o_ref, qaug_ref, s_ref, p_ref, acc_ref):
    tq = blk = MOBA_BLOCK
    n_tiles = qt_ref.shape[2] // tq
    nb = k_ref.shape[1] // blk
    dh = A_HEAD_DIM
    feat = lax.broadcasted_iota(jnp.int32, (LANES, tq), 0)
    kmean = kmean_ref[0]
    blk_id = lax.broadcasted_iota(jnp.int32, (nb, tq), 0)
    lane_k = lax.broadcasted_iota(jnp.int32, (blk, LANES), 1)
    key_i = lax.broadcasted_iota(jnp.int32, (blk, 2 * tq), 0)
    qry_i = lax.broadcasted_iota(jnp.int32, (blk, 2 * tq), 1) & (tq - 1)
    ones_rows = jnp.ones((ACC_ROWS - dh, blk), BF16)

    def fold8(x, op):
        parts = [x[r:r + SUBLANES] for r in range(0, x.shape[0], SUBLANES)]
        while len(parts) > 1:
            parts = [op(a, b) for a, b in zip(parts[0::2], parts[1::2])]
        return parts[0]

    def make_tile(t):
        i = pl.program_id(2) * n_tiles + t
        qaug_t, s_t, p_t, acc_t = qaug_ref.at[t], s_ref.at[t], p_ref.at[t], acc_ref.at[t]

        def put_scores(j, slot):
            st = pl.multiple_of(j * blk, blk)
            onehot = jnp.where(lane_k == j, 1.0, 0.0).astype(BF16)
            k_aug = jnp.concatenate([k_ref[0, pl.ds(st, blk), :], onehot], axis=1)
            sj = _dot(k_aug, qaug_t[...])
            s_t[slot] = sj
            return jnp.max(fold8(sj, jnp.maximum), axis=0, keepdims=True)

        def softmax_into(s_slot, p_slot, m_new):
            rows = 4 * SUBLANES
            for r in range(0, blk, rows):
                x = (s_t[s_slot, r:r + rows, :] - m_new).astype(BF16)
                p_t[p_slot, r:r + rows, :] = jnp.exp2(x)

        def add_pv(slot, vidx, alpha):
            st = pl.multiple_of(vidx * blk, blk)
            for hh in range(2):
                qs = slice(hh * tq, (hh + 1) * tq)
                v_aug = jnp.concatenate([vt_ref[0, hh * dh:(hh + 1) * dh, pl.ds(st, blk)], ones_rows], axis=0)
                acc_t[hh] = alpha[:, qs] * acc_t[hh] + _dot(v_aug, p_t[slot, :, qs])

        def prologue():
            qt = qt_ref[0, :, t * tq:(t + 1) * tq].astype(F32)
            valid = blk_id < i
            q_own, q_aug = [], []
            for hh in range(2):
                in_head = (feat >= hh * dh) & (feat < (hh + 1) * dh)
                qh = jnp.where(in_head, qt, 0.0).astype(BF16)
                g = jnp.where(valid, _dot(kmean, qh), -jnp.inf)
                sel = jnp.zeros(g.shape, jnp.bool_)
                for _ in range(MOBA_TOPK):
                    mx = jnp.max(g, axis=0, keepdims=True)
                    idx = jnp.min(jnp.where(g == mx, blk_id, nb), axis=0, keepdims=True)
                    pick = (blk_id == idx) & valid
                    sel = sel | pick
                    g = jnp.where(pick, -jnp.inf, g)
                bias = jnp.where(sel, 0.0, NEG_BIG).astype(BF16)
                q_own.append(qh)
                q_aug.append(jnp.concatenate([qh, bias, jnp.zeros((LANES - nb, tq), BF16)], axis=0))
            qaug_t[...] = jnp.concatenate(q_aug, axis=1)

            start = pl.multiple_of(i * blk, blk)
            s = _dot(k_ref[0, pl.ds(start, blk), :], jnp.concatenate(q_own, axis=1))
            s_t[1] = jnp.where(key_i <= qry_i, s, NEG_BIG)
            m0 = jnp.max(fold8(s_t[1], jnp.maximum), axis=0, keepdims=True)
            softmax_into(1, 1, m0)
            acc_t[...] = jnp.zeros(acc_t.shape, F32)
            cmax0 = put_scores(0, 0)
            return cmax0, jnp.ones_like(m0), i, m0

        def step(j, slot, state):
            cmax, alpha_pend, vidx, m = state
            cmax_next = put_scores(jnp.minimum(j + 1, nb - 1), 1 - slot)
            add_pv(1 - slot, vidx, alpha_pend)
            m_new = jnp.maximum(m, cmax)
            softmax_into(slot, slot, m_new)
            return cmax_next, jnp.exp2(m - m_new), j, m_new

        def run(first_block, n_trips, unroll, state):
            def body(trip, st):
                for u in range(unroll):
                    st = step(first_block + unroll * trip + u, u % 2, st)
                return st
            return lax.fori_loop(0, n_trips, body, state)

        def loops(state):
            done = 0
            for unroll in MOBA_UNROLLS:
                trips = (i - done) // unroll
                state = run(done, trips, unroll, state)
                done = done + trips * unroll
            return run(done, (i - done + 1) // 2, 2, state)

        def epilogue(state):
            _, alpha_pend, vidx, _ = state
            add_pv(1, vidx, alpha_pend)
            out = jnp.concatenate([acc_t[hh, :dh, :] / acc_t[hh, dh:dh + 1, :] for hh in range(2)],
                                  axis=0)
            o_ref[0, t * tq:(t + 1) * tq, :] = out.T.astype(BF16)

        return prologue, loops, epilogue

    tiles = [make_tile(t) for t in range(n_tiles)]
    states = [prologue() for prologue, _, _ in tiles]
    states = [loops(state) for (_, loops, _), state in zip(tiles, states)]
    for (_, _, epilogue), state in zip(tiles, states):
        epilogue(state)


def _moba(qat, ka, vat, kmean):
    bsz, width, seq = qat.shape
    tq = MOBA_BLOCK
    tw = MOBA_TILES * tq
    npair = width // LANES
    return pl.pallas_call(
        _moba_kernel,
        grid=(bsz, npair, seq // tw),
        in_specs=[pl.BlockSpec((1, LANES, tw), lambda b, p, i: (b, p, i)),
                  pl.BlockSpec((1, seq, LANES), lambda b, p, i: (b, 0, p)),
                  pl.BlockSpec((1, LANES, seq), lambda b, p, i: (b, p, 0)),
                  pl.BlockSpec((1, kmean.shape[1], LANES), lambda b, p, i: (b, 0, p))],
        out_specs=pl.BlockSpec((1, tw, LANES), lambda b, p, i: (b, i, p)),
        out_shape=jax.ShapeDtypeStruct((bsz, seq, width), BF16),
        scratch_shapes=[pltpu.VMEM((MOBA_TILES, 2 * LANES, 2 * tq), BF16),
                        pltpu.VMEM((MOBA_TILES, 2, MOBA_BLOCK, 2 * tq), F32),
                        pltpu.VMEM((MOBA_TILES, 2, MOBA_BLOCK, 2 * tq), BF16),
                        pltpu.VMEM((MOBA_TILES, 2, ACC_ROWS, tq), F32)],
        compiler_params=_cparams(("arbitrary", "arbitrary", "arbitrary")),
        name="moba",
    )(qat, ka, vat, kmean)


def _tail_kernel(x_ref, ym_ref, ya_ref, sga_ref, sgb_ref, mod_ref, g2_ref, gf_ref,
                 pm_ref, pa_ref, wo_ref, wg_ref, wu_ref, wd_ref, o_ref, *, final_norm):
    gate1 = mod_ref[0, 2:3, :]
    sh = mod_ref[0, 3:4, :]
    sc = mod_ref[0, 4:5, :]
    gate = mod_ref[0, 5:6, :]

    merged = (sga_ref[0].astype(F32) * _dot(ym_ref[0], pm_ref[...])
              + sgb_ref[0].astype(F32) * _dot(ya_ref[0], pa_ref[...]))
    x = x_ref[0] + gate1 * _dot(merged.astype(BF16), wo_ref[...])

    h = (_rms(x, g2_ref[...]) * (1.0 + sc) + sh).astype(BF16)
    d_ff = wg_ref.shape[1]
    step = 256
    acc = jnp.zeros(x.shape, F32)
    for c0 in range(0, d_ff, step):
        a = _dot(h, wg_ref[:, c0:c0 + step])
        u = _dot(h, wu_ref[:, c0:c0 + step])
        f = (a * jax.nn.sigmoid(a) * u).astype(BF16)
        acc = acc + _dot(f, wd_ref[c0:c0 + step, :])
    y = x + gate * acc
    o_ref[0] = _rms(y, gf_ref[...]) if final_norm else y


def _tail(x, ym, ya, sga, sgb, mod3, norm2_g, normf_g, weights, final_norm):
    bsz, seq, d = x.shape
    tm = ROW_TILE
    row = lambda b, i: (b, i, 0)
    return pl.pallas_call(
        functools.partial(_tail_kernel, final_norm=final_norm),
        grid=(bsz, seq // tm),
        in_specs=[pl.BlockSpec((1, tm, d), row),
                  pl.BlockSpec((1, tm, ym.shape[2]), row),
                  pl.BlockSpec((1, tm, ya.shape[2]), row),
                  pl.BlockSpec((1, tm, d), row),
                  pl.BlockSpec((1, tm, d), row),
                  pl.BlockSpec((1, N_MOD, d), lambda b, i: (b, 0, 0)),
                  _const_spec((1, d)), _const_spec((1, d)),
                  *[_const_spec(w.shape) for w in weights]],
        out_specs=pl.BlockSpec((1, tm, d), row),
        out_shape=jax.ShapeDtypeStruct((bsz, seq, d), F32),
        compiler_params=_cparams(("arbitrary", "arbitrary")),
        name="merge_ffn_out",
    )(x, ym, ya, sga, sgb, mod3, norm2_g, normf_g, *weights)


def _in_proj_params(w_in, b_in, layer):
    m_end = 4 * M_WIDTH
    if_end = m_end + 2 * M_HEADS
    n_pad = LANES - 2 * M_HEADS
    wt = jnp.swapaxes(w_in[layer], 0, 1)
    b = b_in[layer].reshape(1, -1)
    weights = (wt[:m_end].astype(BF16), wt[if_end:].astype(BF16),
               jnp.pad(wt[m_end:if_end], ((0, n_pad), (0, 0))).astype(BF16))
    biases = (b[:, :m_end], b[:, if_end:], jnp.pad(b[:, m_end:if_end], ((0, 0), (0, n_pad))))
    return weights, biases


def kernel(x, c, positions, ada_w, ada_b, norm1_g, norm2_g, normf_g, w_in, b_in, conv_w, conv_b,
           m_norm_g, p_mlstm, p_moba, w_out, w_gate, w_up, w_down):
    bsz, seq, d = x.shape
    depth = ada_w.shape[0]
    n_blocks = seq // MOBA_BLOCK
    cos_t, sin_t = _rope_tables(positions)
    for l in range(depth):
        mod3 = _ada_mod(c, ada_w, ada_b, l).reshape(bsz, N_MOD, d)
        weights, biases = _in_proj_params(w_in, b_in, l)
        (qm, km, vm, om, qa, ka, va, sga, sgb, zif, kmean_tiles) = _in_proj(
            x, mod3, norm1_g[l].reshape(1, d), weights, biases, conv_w[l],
            conv_b[l].reshape(1, -1), cos_t, sin_t)
        ym = _mlstm(qm, km, vm, om, zif, m_norm_g[l].reshape(1, -1))
        kmean = kmean_tiles[:, :IN_ROW_TILE // MOBA_BLOCK].reshape(bsz, n_blocks, A_WIDTH)
        ya = _moba(qa, ka, va, kmean.astype(BF16))
        tail_weights = [w[l].astype(BF16) for w in (p_mlstm, p_moba, w_out, w_gate, w_up, w_down)]
        x = _tail(x, ym, ya, sga, sgb, mod3, norm2_g[l].reshape(1, d), normf_g.reshape(1, d),
                  tail_weights, final_norm=(l == depth - 1))
    return x
```

```python
import functools

import jax
import jax.numpy as jnp
from jax import lax
from jax.experimental import pallas as pl
from jax.experimental.pallas import tpu as pltpu

F32 = jnp.float32
BF16 = jnp.bfloat16

M_HEADS = 4
M_HEAD_DIM = 128
M_WIDTH = M_HEADS * M_HEAD_DIM
M_CONV = 4
A_HEADS = 8
A_HEAD_DIM = 64
A_WIDTH = A_HEADS * A_HEAD_DIM
MOBA_BLOCK = 256
MOBA_TOPK = 3
ROPE_THETA = 10000.0
N_MOD = 6
RMS_EPS = 1e-6

LANES = 128
SUBLANES = 8
VMEM_LIMIT = 56 * 1024 * 1024

ROW_TILE = 512
IN_ROW_TILE = 512
MLSTM_CHUNK = 256
NEG_BIG = -1e30
LOG2_E = 1.4426950408889634
MOBA_TILES = 8
MOBA_UNROLLS = (8, 4)
ACC_ROWS = A_HEAD_DIM + 16

OFF_MQ, OFF_MK, OFF_MV, OFF_MO = 0, 512, 1024, 1536
OFF_AQ, OFF_AK, OFF_AV = 2048, 2560, 3072
OFF_GA, OFF_GB = 3584, 4608
OFF_IF = 5632
D_IN_PACKED = 5760


def _cparams(sem, flags=None):
    return pltpu.CompilerParams(dimension_semantics=sem, vmem_limit_bytes=VMEM_LIMIT, flags=flags)


def _const_spec(shape):
    nd = len(shape)
    return pl.BlockSpec(shape, lambda *_: (0,) * nd, pipeline_mode=pl.Buffered(1))


def _rms(x, g):
    ms = jnp.mean(x * x, axis=-1, keepdims=True)
    return x * lax.rsqrt(ms + RMS_EPS) * g


def _dot(a, b):
    return jnp.dot(a, b, preferred_element_type=F32)


def _dot_nt(a, b):
    return lax.dot_general(a, b, (((1,), (1,)), ((), ())), preferred_element_type=F32)


def _ada_kernel(c_ref, w_ref, b_ref, o_ref):
    c = c_ref[...]
    ca = c * jax.nn.sigmoid(c)
    o_ref[...] = jnp.dot(ca, w_ref[...], precision=lax.Precision.HIGHEST,
                         preferred_element_type=F32) + b_ref[...]


def _ada_mod(c, ada_w, ada_b, layer):
    bsz, d = c.shape
    depth, _, n = ada_w.shape
    tn = d
    return pl.pallas_call(
        _ada_kernel,
        grid=(n // tn,),
        in_specs=[pl.BlockSpec((bsz, d), lambda j: (0, 0)),
                  pl.BlockSpec((None, d, tn), lambda j: (layer, 0, j)),
                  pl.BlockSpec((None, 1, tn), lambda j: (layer, 0, j))],
        out_specs=pl.BlockSpec((bsz, tn), lambda j: (0, j)),
        out_shape=jax.ShapeDtypeStruct((bsz, n), F32),
        compiler_params=_cparams(("arbitrary",)),
        name="ada_mod",
    )(c, ada_w, ada_b.reshape(depth, 1, n))


def _rope_kernel(pos_ref, freq_ref, cos_ref, sin_ref):
    ang = freq_ref[...] * pos_ref[0].astype(F32)
    c = jnp.cos(ang)
    s = jnp.sin(ang)
    cos_ref[0] = jnp.concatenate([c, c, c, c], axis=0).T
    sin_ref[0] = jnp.concatenate([-s, s, -s, s], axis=0).T


def _rope_tables(positions):
    bsz, seq = positions.shape
    half = A_HEAD_DIM // 2
    inv_freq = (ROPE_THETA ** (-jnp.arange(half, dtype=F32) / half)).reshape(half, 1)
    ts = 2048
    out = jax.ShapeDtypeStruct((bsz, seq, LANES), F32)
    return pl.pallas_call(
        _rope_kernel,
        grid=(bsz, seq // ts),
        in_specs=[pl.BlockSpec((1, 1, ts), lambda b, i: (b, 0, i)),
                  pl.BlockSpec((half, 1), lambda b, i: (0, 0))],
        out_specs=[pl.BlockSpec((1, ts, LANES), lambda b, i: (b, i, 0))] * 2,
        out_shape=[out, out],
        compiler_params=_cparams(("arbitrary", "arbitrary")),
        name="rope_table",
    )(positions.reshape(bsz, 1, seq), inv_freq)


def _in_proj_kernel(x_ref, mod_ref, g_ref, wm_ref, wa_ref, wif_ref, bm_ref, ba_ref, bif_ref,
                    cw_ref, cb_ref, cos_ref, sin_ref,
                    qm_ref, km_ref, vm_ref, om_ref, qa_ref, ka_ref, va_ref,
                    sga_ref, sgb_ref, zif_ref, kmean_ref, hb_ref, zbuf_ref):
    tm = x_ref.shape[1]
    it = pl.program_id(1)

    x = x_ref[0]
    sh = mod_ref[0, 0:1, :]
    sc = mod_ref[0, 1:2, :]
    hb_ref[...] = (_rms(x, g_ref[...]) * (1.0 + sc) + sh).astype(BF16)

    def proj(off, width):
        for base, w_ref, b_ref in ((OFF_IF, wif_ref, bif_ref), (OFF_AQ, wa_ref, ba_ref), (OFF_MQ, wm_ref, bm_ref)):
            if off >= base:
                lo = off - base
                return _dot_nt(hb_ref[...], w_ref[lo:lo + width, :]) + b_ref[:, lo:lo + width]

    @pl.when(it == 0)
    def _():
        zbuf_ref[0:SUBLANES, :] = jnp.zeros((SUBLANES, zbuf_ref.shape[1]), F32)

    @pl.when(it > 0)
    def _():
        zbuf_ref[0:SUBLANES, :] = zbuf_ref[tm:tm + SUBLANES, :]

    zbuf_ref[SUBLANES:SUBLANES + tm, 0:M_WIDTH] = proj(OFF_MQ, M_WIDTH)
    zbuf_ref[SUBLANES:SUBLANES + tm, M_WIDTH:2 * M_WIDTH] = proj(OFF_MK, M_WIDTH)

    def conv_silu(part):
        width = 2 * LANES
        cols = slice(part * width, (part + 1) * width)
        acc = jnp.broadcast_to(cb_ref[:, cols], (tm, width))
        for j in range(M_CONV):
            start = SUBLANES - (M_CONV - 1) + j
            acc = acc + zbuf_ref[start:start + tm, cols] * cw_ref[j:j + 1, cols]
        y = acc * jax.nn.sigmoid(acc)
        if part * width < M_WIDTH:
            qm_ref[0, :, cols] = y.astype(BF16)
        else:
            krows = slice(part * width - M_WIDTH, (part + 1) * width - M_WIDTH)
            km_ref[0, krows, :] = (y * M_HEAD_DIM ** -0.5).T.astype(BF16)

    conv_silu(0)
    vm_ref[0] = proj(OFF_MV, M_WIDTH).astype(BF16)
    conv_silu(1)
    om_ref[0] = proj(OFF_MO, M_WIDTH).astype(BF16)
    conv_silu(2)

    def rope(z):
        outs = []
        for p in range(A_WIDTH // LANES):
            zp = z[:, p * LANES:(p + 1) * LANES]
            lane = lax.broadcasted_iota(jnp.int32, zp.shape, 1)
            first_half = (lane % A_HEAD_DIM) < (A_HEAD_DIM // 2)
            rot = jnp.where(first_half,
                            pltpu.roll(zp, LANES - A_HEAD_DIM // 2, 1),
                            pltpu.roll(zp, A_HEAD_DIM // 2, 1))
            outs.append(zp * cos_ref[0] + rot * sin_ref[0])
        return outs

    q_parts = rope(proj(OFF_AQ, A_WIDTH))
    for p, qp in enumerate(q_parts):
        qa_ref[0, p * LANES:(p + 1) * LANES, :] = (qp * (LOG2_E * A_HEAD_DIM ** -0.5)).T.astype(BF16)
    conv_silu(3)

    k_parts = rope(proj(OFF_AK, A_WIDTH))
    kmean_ref[0] = jnp.zeros(kmean_ref.shape[1:], F32)
    for p, kp in enumerate(k_parts):
        ka_ref[0, :, p * LANES:(p + 1) * LANES] = kp.astype(BF16)
        for blk in range(tm // MOBA_BLOCK):
            kmean_ref[0, blk:blk + 1, p * LANES:(p + 1) * LANES] = jnp.mean(
                kp[blk * MOBA_BLOCK:(blk + 1) * MOBA_BLOCK], axis=0, keepdims=True)

    va = proj(OFF_AV, A_WIDTH)
    for p in range(A_WIDTH // LANES):
        va_ref[0, p * LANES:(p + 1) * LANES, :] = va[:, p * LANES:(p + 1) * LANES].T.astype(BF16)

    d = x.shape[1]
    for off, out_ref in ((OFF_GA, sga_ref), (OFF_GB, sgb_ref)):
        for c0 in range(0, d, 512):
            out_ref[0, :, c0:c0 + 512] = jax.nn.sigmoid(proj(off + c0, 512)).astype(BF16)

    zif_ref[0] = proj(OFF_IF, LANES)


def _in_proj(x, mod3, norm_g, weights, biases, conv_w, conv_b, cos_t, sin_t):
    bsz, seq, d = x.shape
    tm = IN_ROW_TILE
    nt = seq // tm
    row = lambda b, i: (b, i, 0)

    def rows(width, dtype):
        return (pl.BlockSpec((1, tm, width), row), jax.ShapeDtypeStruct((bsz, seq, width), dtype))

    cols = (pl.BlockSpec((1, A_WIDTH, tm), lambda b, i: (b, 0, i)),
            jax.ShapeDtypeStruct((bsz, A_WIDTH, seq), BF16))
    outs = ([rows(M_WIDTH, BF16), cols] + [rows(M_WIDTH, BF16)] * 2 + [cols, rows(A_WIDTH, BF16), cols]
            + [rows(d, BF16)] * 2 + [rows(LANES, F32)])
    outs.append((pl.BlockSpec((1, SUBLANES, A_WIDTH), lambda b, i: (b * nt + i, 0, 0)),
                 jax.ShapeDtypeStruct((bsz * nt, SUBLANES, A_WIDTH), F32)))
    return pl.pallas_call(
        _in_proj_kernel,
        grid=(bsz, nt),
        in_specs=[pl.BlockSpec((1, tm, d), row),
                  pl.BlockSpec((1, N_MOD, d), lambda b, i: (b, 0, 0)),
                  _const_spec((1, d)),
                  *[_const_spec(t.shape) for t in (*weights, *biases)],
                  _const_spec(conv_w.shape),
                  _const_spec(conv_b.shape),
                  pl.BlockSpec((1, tm, LANES), row),
                  pl.BlockSpec((1, tm, LANES), row)],
        out_specs=[o[0] for o in outs],
        out_shape=[o[1] for o in outs],
        scratch_shapes=[pltpu.VMEM((tm, d), BF16),
                        pltpu.VMEM((tm + SUBLANES, 2 * M_WIDTH), F32)],
        compiler_params=_cparams(("arbitrary", "arbitrary")),
        name="in_proj",
    )(x, mod3, norm_g, *weights, *biases, conv_w, conv_b, cos_t, sin_t)


def _split3(x):
    hi = x.astype(BF16)
    r1 = x - hi.astype(F32)
    mid = r1.astype(BF16)
    lo = (r1 - mid.astype(F32)).astype(BF16)
    return hi, mid, lo


def _mlstm_kernel(q_ref, kt_ref, v_ref, o_ref, zif_ref, g_ref, ym_ref, ct_ref, m_ref, s_ref, sw_ref, mt_ref):
    bsz, L = q_ref.shape[0], q_ref.shape[1]
    dh = M_HEAD_DIM

    @pl.when(pl.program_id(0) == 0)
    def _():
        ct_ref[...] = jnp.zeros(ct_ref.shape, F32)
        m_ref[...] = jnp.zeros(m_ref.shape, F32)

    r_i = lax.broadcasted_iota(jnp.int32, (L, L), 0)
    c_i = lax.broadcasted_iota(jnp.int32, (L, L), 1)
    causal = c_i <= r_i
    triu = jnp.where(r_i <= c_i, 1.0, 0.0).astype(BF16)
    ones = jnp.ones((L, dh), BF16)
    gate_row = lax.broadcasted_iota(jnp.int32, (SUBLANES, L), 0)

    gate_logs = []
    for b in range(bsz):
        gates = zif_ref[b].T[0:SUBLANES, :]
        log_f = jnp.minimum(gates, 0.0) - jnp.log1p(jnp.exp(-jnp.abs(gates)))
        log_f = jnp.where(gate_row >= M_HEADS, log_f, 0.0)
        bcum = sum(_dot(part, triu) for part in _split3(log_f))
        xrow = jnp.where(gate_row < M_HEADS, gates, bcum) * LOG2_E
        xcol = jnp.concatenate([xrow, jnp.zeros((LANES - SUBLANES, L), F32)], axis=0).T
        gate_logs.append((xrow, xcol))

    chains = [(b, h) for b in range(bsz) for h in range(M_HEADS)]

    def gate_terms(b, h):
        xrow, xcol = gate_logs[b]
        li_col = xcol[:, h:h + 1]
        b_col = xcol[:, M_HEADS + h:M_HEADS + h + 1]
        li_row = xrow[h:h + 1, :]
        b_row = xrow[M_HEADS + h:M_HEADS + h + 1, :]
        return li_col, b_col, li_row, b_row, b_col[L - 1:L, :]

    def v_aug(b, h):
        return jnp.concatenate([v_ref[b, :, h * dh:(h + 1) * dh], ones], axis=1)

    for st, (b, h) in enumerate(chains):
        hs = slice(h * dh, (h + 1) * dh)
        s_ref[st] = _dot(q_ref[b, :, hs], kt_ref[b, hs, :])

    for st, (b, h) in enumerate(chains):
        _, b_col, li_row, b_row, _ = gate_terms(b, h)
        dmat = jnp.where(causal, b_col + (li_row - b_row), -jnp.inf)
        inter = b_col + m_ref[st:st + 1, 0:1]
        m_t = jnp.maximum(inter, jnp.max(dmat, axis=1, keepdims=True))
        sw_ref[st] = (s_ref[st] * jnp.exp2(dmat - m_t)).astype(BF16)
        mt_ref[st, 0] = jnp.broadcast_to(m_t, (L, LANES))
        mt_ref[st, 1] = jnp.broadcast_to(inter - m_t, (L, LANES))

    for st, (b, h) in enumerate(chains):
        hs = slice(h * dh, (h + 1) * dh)
        m_t = mt_ref[st, 0]
        inter_w = jnp.exp2(mt_ref[st, 1])
        qc = _dot(q_ref[b, :, hs], ct_ref[st].astype(BF16))
        nd = jnp.concatenate([inter_w, inter_w], axis=1) * qc + _dot(sw_ref[st], v_aug(b, h))
        hval = nd[:, :dh] / jnp.maximum(jnp.abs(nd[:, dh:]), jnp.exp2(-m_t))
        hn = _rms(hval, g_ref[:, hs])
        ym_ref[b, :, hs] = (hn * jax.nn.sigmoid(o_ref[b, :, hs].astype(F32))).astype(BF16)

    for st, (b, h) in enumerate(chains):
        hs = slice(h * dh, (h + 1) * dh)
        _, _, li_row, b_row, b_last = gate_terms(b, h)
        m_prev = m_ref[st:st + 1, 0:1]
        a_row = b_last - b_row + li_row
        m_new = jnp.maximum(b_last + m_prev, jnp.max(a_row, axis=1, keepdims=True))
        kw_t = (kt_ref[b, hs, :].astype(F32) * jnp.exp2(a_row - m_new)).astype(BF16)
        ct_ref[st] = jnp.exp2(b_last + m_prev - m_new) * ct_ref[st] + _dot(kw_t, v_aug(b, h))
        m_ref[st:st + 1, :] = jnp.broadcast_to(m_new, (1, LANES))


def _mlstm(qm, kmt, vm, om, zif, m_norm_g):
    bsz, seq, width = qm.shape
    L = MLSTM_CHUNK
    row = lambda c: (0, c, 0)
    rows = pl.BlockSpec((bsz, L, width), row)
    return pl.pallas_call(
        _mlstm_kernel,
        grid=(seq // L,),
        in_specs=[rows, pl.BlockSpec((bsz, width, L), lambda c: (0, 0, c)), rows, rows,
                  pl.BlockSpec((bsz, L, LANES), row), _const_spec((1, width))],
        out_specs=pl.BlockSpec((bsz, L, width), row),
        out_shape=jax.ShapeDtypeStruct((bsz, seq, width), BF16),
        scratch_shapes=[pltpu.VMEM((bsz * M_HEADS, M_HEAD_DIM, 2 * M_HEAD_DIM), F32),
                        pltpu.VMEM((bsz * M_HEADS, LANES), F32),
                        pltpu.VMEM((bsz * M_HEADS, L, L), F32),
                        pltpu.VMEM((bsz * M_HEADS, L, L), BF16),
                        pltpu.VMEM((bsz * M_HEADS, 2, L, LANES), F32)],
        compiler_params=_cparams(("arbitrary",)),
        name="mlstm",
    )(qm, kmt, vm, om, zif, m_norm_g)


def _moba_kernel(qt_ref, k_ref, vt_ref, kmean_ref, o_ref, qaug_ref, s_ref, p_ref, acc_ref):
    tq = blk = MOBA_BLOCK
    n_tiles = qt_ref.shape[2] // tq
    nb = k_ref.shape[1] // blk
    dh = A_HEAD_DIM
    feat = lax.broadcasted_iota(jnp.int32, (LANES, tq), 0)
    kmean = kmean_ref[0]
    blk_id = lax.broadcasted_iota(jnp.int32, (nb, tq), 0)
    lane_k = lax.broadcasted_iota(jnp.int32, (blk, LANES), 1)
    key_i = lax.broadcasted_iota(jnp.int32, (blk, 2 * tq), 0)
    qry_i = lax.broadcasted_iota(jnp.int32, (blk, 2 * tq), 1) & (tq - 1)
    ones_rows = jnp.ones((ACC_ROWS - dh, blk), BF16)

    def fold8(x, op):
        parts = [x[r:r + SUBLANES] for r in range(0, x.shape[0], SUBLANES)]
        while len(parts) > 1:
            parts = [op(a, b) for a, b in zip(parts[0::2], parts[1::2])]
        return parts[0]

    def make_tile(t):
        i = pl.program_id(2) * n_tiles + t
        qaug_t, s_t, p_t, acc_t = qaug_ref.at[t], s_ref.at[t], p_ref.at[t], acc_ref.at[t]

        def put_scores(j, slot):
            st = pl.multiple_of(j * blk, blk)
            onehot = jnp.where(lane_k == j, 1.0, 0.0).astype(BF16)
            k_aug = jnp.concatenate([k_ref[0, pl.ds(st, blk), :], onehot], axis=1)
            sj = _dot(k_aug, qaug_t[...])
            s_t[slot] = sj
            return jnp.max(fold8(sj, jnp.maximum), axis=0, keepdims=True)

        def softmax_into(s_slot, p_slot, m_new):
            rows = 4 * SUBLANES
            for r in range(0, blk, rows):
                x = (s_t[s_slot, r:r + rows, :] - m_new).astype(BF16)
                p_t[p_slot, r:r + rows, :] = jnp.exp2(x)

        def add_pv(slot, vidx, alpha):
            st = pl.multiple_of(vidx * blk, blk)
            for hh in range(2):
                qs = slice(hh * tq, (hh + 1) * tq)
                v_aug = jnp.concatenate([vt_ref[0, hh * dh:(hh + 1) * dh, pl.ds(st, blk)], ones_rows], axis=0)
                acc_t[hh] = alpha[:, qs] * acc_t[hh] + _dot(v_aug, p_t[slot, :, qs])

        def prologue():
            qt = qt_ref[0, :, t * tq:(t + 1) * tq].astype(F32)
            valid = blk_id < i
            q_own, q_aug = [], []
            for hh in range(2):
                in_head = (feat >= hh * dh) & (feat < (hh + 1) * dh)
                qh = jnp.where(in_head, qt, 0.0).astype(BF16)
                g = jnp.where(valid, _dot(kmean, qh), -jnp.inf)
                sel = jnp.zeros(g.shape, jnp.bool_)
                for _ in range(MOBA_TOPK):
                    mx = jnp.max(g, axis=0, keepdims=True)
                    idx = jnp.min(jnp.where(g == mx, blk_id, nb), axis=0, keepdims=True)
                    pick = (blk_id == idx) & valid
                    sel = sel | pick
                    g = jnp.where(pick, -jnp.inf, g)
                bias = jnp.where(sel, 0.0, NEG_BIG).astype(BF16)
                q_own.append(qh)
                q_aug.append(jnp.concatenate([qh, bias, jnp.zeros((LANES - nb, tq), BF16)], axis=0))
            qaug_t[...] = jnp.concatenate(q_aug, axis=1)

            start = pl.multiple_of(i * blk, blk)
            s = _dot(k_ref[0, pl.ds(start, blk), :], jnp.concatenate(q_own, axis=1))
            s_t[1] = jnp.where(key_i <= qry_i, s, NEG_BIG)
            m0 = jnp.max(fold8(s_t[1], jnp.maximum), axis=0, keepdims=True)
            softmax_into(1, 1, m0)
            acc_t[...] = jnp.zeros(acc_t.shape, F32)
            cmax0 = put_scores(0, 0)
            return cmax0, jnp.ones_like(m0), i, m0

        def step(j, slot, state):
            cmax, alpha_pend, vidx, m = state
            cmax_next = put_scores(jnp.minimum(j + 1, nb - 1), 1 - slot)
            add_pv(1 - slot, vidx, alpha_pend)
            m_new = jnp.maximum(m, cmax)
            softmax_into(slot, slot, m_new)
            return cmax_next, jnp.exp2(m - m_new), j, m_new

        def run(first_block, n_trips, unroll, state):
            def body(trip, st):
                for u in range(unroll):
                    st = step(first_block + unroll * trip + u, u % 2, st)
                return st
            return lax.fori_loop(0, n_trips, body, state)

        def loops(state):
            done = 0
            for unroll in MOBA_UNROLLS:
                trips = (i - done) // unroll
                state = run(done, trips, unroll, state)
                done = done + trips * unroll
            return run(done, (i - done + 1) // 2, 2, state)

        def epilogue(state):
            _, alpha_pend, vidx, _ = state
            add_pv(1, vidx, alpha_pend)
            out = jnp.concatenate([acc_t[hh, :dh, :] / acc_t[hh, dh:dh + 1, :] for hh in range(2)],
                                  axis=0)
            o_ref[0, t * tq:(t + 1) * tq, :] = out.T.astype(BF16)

        return prologue, loops, epilogue

    tiles = [make_tile(t) for t in range(n_tiles)]
    states = [prologue() for prologue, _, _ in tiles]
    states = [loops(state) for (_, loops, _), state in zip(tiles, states)]
    for (_, _, epilogue), state in zip(tiles, states):
        epilogue(state)


def _moba(qat, ka, vat, kmean):
    bsz, width, seq = qat.shape
    tq = MOBA_BLOCK
    tw = MOBA_TILES * tq
    npair = width // LANES
    return pl.pallas_call(
        _moba_kernel,
        grid=(bsz, npair, seq // tw),
        in_specs=[pl.BlockSpec((1, LANES, tw), lambda b, p, i: (b, p, i)),
                  pl.BlockSpec((1, seq, LANES), lambda b, p, i: (b, 0, p)),
                  pl.BlockSpec((1, LANES, seq), lambda b, p, i: (b, p, 0)),
                  pl.BlockSpec((1, kmean.shape[1], LANES), lambda b, p, i: (b, 0, p))],
        out_specs=pl.BlockSpec((1, tw, LANES), lambda b, p, i: (b, i, p)),
        out_shape=jax.ShapeDtypeStruct((bsz, seq, width), BF16),
        scratch_shapes=[pltpu.VMEM((MOBA_TILES, 2 * LANES, 2 * tq), BF16),
                        pltpu.VMEM((MOBA_TILES, 2, MOBA_BLOCK, 2 * tq), F32),
                        pltpu.VMEM((MOBA_TILES, 2, MOBA_BLOCK, 2 * tq), BF16),
                        pltpu.VMEM((MOBA_TILES, 2, ACC_ROWS, tq), F32)],
        compiler_params=_cparams(("arbitrary", "arbitrary", "arbitrary")),
        name="moba",
    )(qat, ka, vat, kmean)


def _tail_kernel(x_ref, ym_ref, ya_ref, sga_ref, sgb_ref, mod_ref, g2_ref, gf_ref,
                 pm_ref, pa_ref, wo_ref, wg_ref, wu_ref, wd_ref, o_ref, *, final_norm):
    gate1 = mod_ref[0, 2:3, :]
    sh = mod_ref[0, 3:4, :]
    sc = mod_ref[0, 4:5, :]
    gate = mod_ref[0, 5:6, :]

    merged = (sga_ref[0].astype(F32) * _dot(ym_ref[0], pm_ref[...])
              + sgb_ref[0].astype(F32) * _dot(ya_ref[0], pa_ref[...]))
    x = x_ref[0] + gate1 * _dot(merged.astype(BF16), wo_ref[...])

    h = (_rms(x, g2_ref[...]) * (1.0 + sc) + sh).astype(BF16)
    d_ff = wg_ref.shape[1]
    step = 256
    acc = jnp.zeros(x.shape, F32)
    for c0 in range(0, d_ff, step):
        a = _dot(h, wg_ref[:, c0:c0 + step])
        u = _dot(h, wu_ref[:, c0:c0 + step])
        f = (a * jax.nn.sigmoid(a) * u).astype(BF16)
        acc = acc + _dot(f, wd_ref[c0:c0 + step, :])
    y = x + gate * acc
    o_ref[0] = _rms(y, gf_ref[...]) if final_norm else y


def _tail(x, ym, ya, sga, sgb, mod3, norm2_g, normf_g, weights, final_norm):
    bsz, seq, d = x.shape
    tm = ROW_TILE
    row = lambda b, i: (b, i, 0)
    return pl.pallas_call(
        functools.partial(_tail_kernel, final_norm=final_norm),
        grid=(bsz, seq // tm),
        in_specs=[pl.BlockSpec((1, tm, d), row),
                  pl.BlockSpec((1, tm, ym.shape[2]), row),
                  pl.BlockSpec((1, tm, ya.shape[2]), row),
                  pl.BlockSpec((1, tm, d), row),
                  pl.BlockSpec((1, tm, d), row),
                  pl.BlockSpec((1, N_MOD, d), lambda b, i: (b, 0, 0)),
                  _const_spec((1, d)), _const_spec((1, d)),
                  *[_const_spec(w.shape) for w in weights]],
        out_specs=pl.BlockSpec((1, tm, d), row),
        out_shape=jax.ShapeDtypeStruct((bsz, seq, d), F32),
        compiler_params=_cparams(("arbitrary", "arbitrary")),
        name="merge_ffn_out",
    )(x, ym, ya, sga, sgb, mod3, norm2_g, normf_g, *weights)


CAST_STEPS = 8


def _cast_kernel(*refs):
    n = len(refs) // 2
    for src, dst in zip(refs[:n], refs[n:]):
        dst[...] = src[...].astype(BF16)


def _cast_layer_bf16(stacked, layer):
    in_specs, out_specs, out_shape = [], [], []
    for w in stacked:
        _, rows, cols = w.shape
        assert rows % (16 * CAST_STEPS) == 0
        tr = rows // CAST_STEPS
        in_specs.append(pl.BlockSpec((None, tr, cols), lambda i: (layer, i, 0)))
        out_specs.append(pl.BlockSpec((tr, cols), lambda i: (i, 0)))
        out_shape.append(jax.ShapeDtypeStruct((rows, cols), BF16))
    return pl.pallas_call(
        _cast_kernel,
        grid=(CAST_STEPS,),
        in_specs=in_specs,
        out_specs=out_specs,
        out_shape=out_shape,
        compiler_params=_cparams(("arbitrary",)),
        name="cast_weights",
    )(*stacked)


def _in_proj_params(w_in, b_in, layer):
    m_end = 4 * M_WIDTH
    if_end = m_end + 2 * M_HEADS
    n_pad = LANES - 2 * M_HEADS
    wt = jnp.swapaxes(w_in[layer], 0, 1)
    b = b_in[layer].reshape(1, -1)
    weights = (wt[:m_end].astype(BF16), wt[if_end:].astype(BF16),
               jnp.pad(wt[m_end:if_end], ((0, n_pad), (0, 0))).astype(BF16))
    biases = (b[:, :m_end], b[:, if_end:], jnp.pad(b[:, m_end:if_end], ((0, 0), (0, n_pad))))
    return weights, biases


def kernel(x, c, positions, ada_w, ada_b, norm1_g, norm2_g, normf_g, w_in, b_in, conv_w, conv_b,
           m_norm_g, p_mlstm, p_moba, w_out, w_gate, w_up, w_down):
    bsz, seq, d = x.shape
    depth = ada_w.shape[0]
    n_blocks = seq // MOBA_BLOCK
    cos_t, sin_t = _rope_tables(positions)
    for l in range(depth):
        mod3 = _ada_mod(c, ada_w, ada_b, l).reshape(bsz, N_MOD, d)
        weights, biases = _in_proj_params(w_in, b_in, l)
        (qm, km, vm, om, qa, ka, va, sga, sgb, zif, kmean_tiles) = _in_proj(
            x, mod3, norm1_g[l].reshape(1, d), weights, biases, conv_w[l],
            conv_b[l].reshape(1, -1), cos_t, sin_t)
        ym = _mlstm(qm, km, vm, om, zif, m_norm_g[l].reshape(1, -1))
        kmean = kmean_tiles[:, :IN_ROW_TILE // MOBA_BLOCK].reshape(bsz, n_blocks, A_WIDTH)
        ya = _moba(qa, ka, va, kmean.astype(BF16))
        tail_weights = _cast_layer_bf16((p_mlstm, p_moba, w_out, w_gate, w_up, w_down), l)
        x = _tail(x, ym, ya, sga, sgb, mod3, norm2_g[l].reshape(1, d), normf_g.reshape(1, d),
                  tail_weights, final_norm=(l == depth - 1))
    return x
```

```python
import functools

import jax
import jax.numpy as jnp
from jax import lax
from jax.experimental import pallas as pl
from jax.experimental.pallas import tpu as pltpu

F32 = jnp.float32
BF16 = jnp.bfloat16

M_HEADS = 4
M_HEAD_DIM = 128
M_WIDTH = M_HEADS * M_HEAD_DIM
M_CONV = 4
A_HEADS = 8
A_HEAD_DIM = 64
A_WIDTH = A_HEADS * A_HEAD_DIM
MOBA_BLOCK = 256
MOBA_TOPK = 3
ROPE_THETA = 10000.0
N_MOD = 6
RMS_EPS = 1e-6

LANES = 128
SUBLANES = 8
VMEM_LIMIT = 56 * 1024 * 1024

ROW_TILE = 512
IN_ROW_TILE = 512
MLSTM_CHUNK = 256
NEG_BIG = -1e30
LOG2_E = 1.4426950408889634
MOBA_TILES = 8
MOBA_SCORE_SLOTS = 4
MOBA_UNROLLS = (8, 4)
ACC_ROWS = A_HEAD_DIM + 16

OFF_MQ, OFF_MK, OFF_MV, OFF_MO = 0, 512, 1024, 1536
OFF_AQ, OFF_AK, OFF_AV = 2048, 2560, 3072
OFF_GA, OFF_GB = 3584, 4608
OFF_IF = 5632
D_IN_PACKED = 5760


def _cparams(sem, flags=None):
    return pltpu.CompilerParams(dimension_semantics=sem, vmem_limit_bytes=VMEM_LIMIT, flags=flags)


def _const_spec(shape):
    nd = len(shape)
    return pl.BlockSpec(shape, lambda *_: (0,) * nd, pipeline_mode=pl.Buffered(1))


def _rms(x, g):
    ms = jnp.mean(x * x, axis=-1, keepdims=True)
    return x * lax.rsqrt(ms + RMS_EPS) * g


def _dot(a, b):
    return jnp.dot(a, b, preferred_element_type=F32)


def _dot_nt(a, b):
    return lax.dot_general(a, b, (((1,), (1,)), ((), ())), preferred_element_type=F32)


def _ada_kernel(c_ref, w_ref, b_ref, o_ref):
    c = c_ref[...]
    ca = c * jax.nn.sigmoid(c)
    o_ref[...] = jnp.dot(ca, w_ref[...], precision=lax.Precision.HIGHEST,
                         preferred_element_type=F32) + b_ref[...]


def _ada_mod(c, ada_w, ada_b, layer):
    bsz, d = c.shape
    depth, _, n = ada_w.shape
    tn = d
    return pl.pallas_call(
        _ada_kernel,
        grid=(n // tn,),
        in_specs=[pl.BlockSpec((bsz, d), lambda j: (0, 0)),
                  pl.BlockSpec((None, d, tn), lambda j: (layer, 0, j)),
                  pl.BlockSpec((None, 1, tn), lambda j: (layer, 0, j))],
        out_specs=pl.BlockSpec((bsz, tn), lambda j: (0, j)),
        out_shape=jax.ShapeDtypeStruct((bsz, n), F32),
        compiler_params=_cparams(("arbitrary",)),
        name="ada_mod",
    )(c, ada_w, ada_b.reshape(depth, 1, n))


def _rope_kernel(pos_ref, freq_ref, cos_ref, sin_ref):
    ang = freq_ref[...] * pos_ref[0].astype(F32)
    c = jnp.cos(ang)
    s = jnp.sin(ang)
    cos_ref[0] = jnp.concatenate([c, c, c, c], axis=0).T
    sin_ref[0] = jnp.concatenate([-s, s, -s, s], axis=0).T


def _rope_tables(positions):
    bsz, seq = positions.shape
    half = A_HEAD_DIM // 2
    inv_freq = (ROPE_THETA ** (-jnp.arange(half, dtype=F32) / half)).reshape(half, 1)
    ts = 2048
    out = jax.ShapeDtypeStruct((bsz, seq, LANES), F32)
    return pl.pallas_call(
        _rope_kernel,
        grid=(bsz, seq // ts),
        in_specs=[pl.BlockSpec((1, 1, ts), lambda b, i: (b, 0, i)),
                  pl.BlockSpec((half, 1), lambda b, i: (0, 0))],
        out_specs=[pl.BlockSpec((1, ts, LANES), lambda b, i: (b, i, 0))] * 2,
        out_shape=[out, out],
        compiler_params=_cparams(("arbitrary", "arbitrary")),
        name="rope_table",
    )(positions.reshape(bsz, 1, seq), inv_freq)


def _in_proj_kernel(x_ref, mod_ref, g_ref, wm_ref, wa_ref, wif_ref, bm_ref, ba_ref, bif_ref,
                    cw_ref, cb_ref, cos_ref, sin_ref,
                    qm_ref, km_ref, vm_ref, om_ref, qa_ref, ka_ref, va_ref,
                    sga_ref, sgb_ref, zif_ref, kmean_ref, hb_ref, zbuf_ref):
    tm = x_ref.shape[1]
    it = pl.program_id(1)

    x = x_ref[0]
    sh = mod_ref[0, 0:1, :]
    sc = mod_ref[0, 1:2, :]
    hb_ref[...] = (_rms(x, g_ref[...]) * (1.0 + sc) + sh).astype(BF16)

    def proj(off, width):
        for base, w_ref, b_ref in ((OFF_IF, wif_ref, bif_ref), (OFF_AQ, wa_ref, ba_ref), (OFF_MQ, wm_ref, bm_ref)):
            if off >= base:
                lo = off - base
                return _dot_nt(hb_ref[...], w_ref[lo:lo + width, :]) + b_ref[:, lo:lo + width]

    @pl.when(it == 0)
    def _():
        zbuf_ref[0:SUBLANES, :] = jnp.zeros((SUBLANES, zbuf_ref.shape[1]), F32)

    @pl.when(it > 0)
    def _():
        zbuf_ref[0:SUBLANES, :] = zbuf_ref[tm:tm + SUBLANES, :]

    zbuf_ref[SUBLANES:SUBLANES + tm, 0:M_WIDTH] = proj(OFF_MQ, M_WIDTH)
    zbuf_ref[SUBLANES:SUBLANES + tm, M_WIDTH:2 * M_WIDTH] = proj(OFF_MK, M_WIDTH)

    def conv_silu(part):
        width = 2 * LANES
        cols = slice(part * width, (part + 1) * width)
        acc = jnp.broadcast_to(cb_ref[:, cols], (tm, width))
        for j in range(M_CONV):
            start = SUBLANES - (M_CONV - 1) + j
            acc = acc + zbuf_ref[start:start + tm, cols] * cw_ref[j:j + 1, cols]
        y = acc * jax.nn.sigmoid(acc)
        if part * width < M_WIDTH:
            qm_ref[0, :, cols] = y.astype(BF16)
        else:
            krows = slice(part * width - M_WIDTH, (part + 1) * width - M_WIDTH)
            km_ref[0, krows, :] = (y * M_HEAD_DIM ** -0.5).T.astype(BF16)

    conv_silu(0)
    vm_ref[0] = proj(OFF_MV, M_WIDTH).astype(BF16)
    conv_silu(1)
    om_ref[0] = proj(OFF_MO, M_WIDTH).astype(BF16)
    conv_silu(2)

    def rope(z):
        outs = []
        for p in range(A_WIDTH // LANES):
            zp = z[:, p * LANES:(p + 1) * LANES]
            lane = lax.broadcasted_iota(jnp.int32, zp.shape, 1)
            first_half = (lane % A_HEAD_DIM) < (A_HEAD_DIM // 2)
            rot = jnp.where(first_half,
                            pltpu.roll(zp, LANES - A_HEAD_DIM // 2, 1),
                            pltpu.roll(zp, A_HEAD_DIM // 2, 1))
            outs.append(zp * cos_ref[0] + rot * sin_ref[0])
        return outs

    q_parts = rope(proj(OFF_AQ, A_WIDTH))
    for p, qp in enumerate(q_parts):
        qa_ref[0, p * LANES:(p + 1) * LANES, :] = (qp * (LOG2_E * A_HEAD_DIM ** -0.5)).T.astype(BF16)
    conv_silu(3)

    k_parts = rope(proj(OFF_AK, A_WIDTH))
    kmean_ref[0] = jnp.zeros(kmean_ref.shape[1:], F32)
    for p, kp in enumerate(k_parts):
        ka_ref[0, :, p * LANES:(p + 1) * LANES] = kp.astype(BF16)
        for blk in range(tm // MOBA_BLOCK):
            kmean_ref[0, blk:blk + 1, p * LANES:(p + 1) * LANES] = jnp.mean(
                kp[blk * MOBA_BLOCK:(blk + 1) * MOBA_BLOCK], axis=0, keepdims=True)

    va = proj(OFF_AV, A_WIDTH)
    for p in range(A_WIDTH // LANES):
        va_ref[0, p * LANES:(p + 1) * LANES, :] = va[:, p * LANES:(p + 1) * LANES].T.astype(BF16)

    d = x.shape[1]
    for off, out_ref in ((OFF_GA, sga_ref), (OFF_GB, sgb_ref)):
        for c0 in range(0, d, 512):
            out_ref[0, :, c0:c0 + 512] = jax.nn.sigmoid(proj(off + c0, 512)).astype(BF16)

    zif_ref[0] = proj(OFF_IF, LANES)


def _in_proj(x, mod3, norm_g, weights, biases, conv_w, conv_b, cos_t, sin_t):
    bsz, seq, d = x.shape
    tm = IN_ROW_TILE
    nt = seq // tm
    row = lambda b, i: (b, i, 0)

    def rows(width, dtype):
        return (pl.BlockSpec((1, tm, width), row), jax.ShapeDtypeStruct((bsz, seq, width), dtype))

    cols = (pl.BlockSpec((1, A_WIDTH, tm), lambda b, i: (b, 0, i)),
            jax.ShapeDtypeStruct((bsz, A_WIDTH, seq), BF16))
    outs = ([rows(M_WIDTH, BF16), cols] + [rows(M_WIDTH, BF16)] * 2 + [cols, rows(A_WIDTH, BF16), cols]
            + [rows(d, BF16)] * 2 + [rows(LANES, F32)])
    outs.append((pl.BlockSpec((1, SUBLANES, A_WIDTH), lambda b, i: (b * nt + i, 0, 0)),
                 jax.ShapeDtypeStruct((bsz * nt, SUBLANES, A_WIDTH), F32)))
    return pl.pallas_call(
        _in_proj_kernel,
        grid=(bsz, nt),
        in_specs=[pl.BlockSpec((1, tm, d), row),
                  pl.BlockSpec((1, N_MOD, d), lambda b, i: (b, 0, 0)),
                  _const_spec((1, d)),
                  *[_const_spec(t.shape) for t in (*weights, *biases)],
                  _const_spec(conv_w.shape),
                  _const_spec(conv_b.shape),
                  pl.BlockSpec((1, tm, LANES), row),
                  pl.BlockSpec((1, tm, LANES), row)],
        out_specs=[o[0] for o in outs],
        out_shape=[o[1] for o in outs],
        scratch_shapes=[pltpu.VMEM((tm, d), BF16),
                        pltpu.VMEM((tm + SUBLANES, 2 * M_WIDTH), F32)],
        compiler_params=_cparams(("arbitrary", "arbitrary")),
        name="in_proj",
    )(x, mod3, norm_g, *weights, *biases, conv_w, conv_b, cos_t, sin_t)


def _split3(x):
    hi = x.astype(BF16)
    r1 = x - hi.astype(F32)
    mid = r1.astype(BF16)
    lo = (r1 - mid.astype(F32)).astype(BF16)
    return hi, mid, lo


def _mlstm_kernel(q_ref, kt_ref, v_ref, o_ref, zif_ref, g_ref, ym_ref, ct_ref, m_ref, s_ref, sw_ref, mt_ref):
    bsz, L = q_ref.shape[0], q_ref.shape[1]
    dh = M_HEAD_DIM

    @pl.when(pl.program_id(0) == 0)
    def _():
        ct_ref[...] = jnp.zeros(ct_ref.shape, F32)
        m_ref[...] = jnp.zeros(m_ref.shape, F32)

    r_i = lax.broadcasted_iota(jnp.int32, (L, L), 0)
    c_i = lax.broadcasted_iota(jnp.int32, (L, L), 1)
    causal = c_i <= r_i
    triu = jnp.where(r_i <= c_i, 1.0, 0.0).astype(BF16)
    ones = jnp.ones((L, dh), BF16)
    gate_row = lax.broadcasted_iota(jnp.int32, (SUBLANES, L), 0)

    gate_logs = []
    for b in range(bsz):
        gates = zif_ref[b].T[0:SUBLANES, :]
        log_f = jnp.minimum(gates, 0.0) - jnp.log1p(jnp.exp(-jnp.abs(gates)))
        log_f = jnp.where(gate_row >= M_HEADS, log_f, 0.0)
        bcum = sum(_dot(part, triu) for part in _split3(log_f))
        xrow = jnp.where(gate_row < M_HEADS, gates, bcum) * LOG2_E
        xcol = jnp.concatenate([xrow, jnp.zeros((LANES - SUBLANES, L), F32)], axis=0).T
        gate_logs.append((xrow, xcol))

    chains = [(b, h) for b in range(bsz) for h in range(M_HEADS)]

    def gate_terms(b, h):
        xrow, xcol = gate_logs[b]
        li_col = xcol[:, h:h + 1]
        b_col = xcol[:, M_HEADS + h:M_HEADS + h + 1]
        li_row = xrow[h:h + 1, :]
        b_row = xrow[M_HEADS + h:M_HEADS + h + 1, :]
        return li_col, b_col, li_row, b_row, b_col[L - 1:L, :]

    def v_aug(b, h):
        return jnp.concatenate([v_ref[b, :, h * dh:(h + 1) * dh], ones], axis=1)

    for st, (b, h) in enumerate(chains):
        hs = slice(h * dh, (h + 1) * dh)
        s_ref[st] = _dot(q_ref[b, :, hs], kt_ref[b, hs, :])

    for st, (b, h) in enumerate(chains):
        _, b_col, li_row, b_row, _ = gate_terms(b, h)
        dmat = jnp.where(causal, b_col + (li_row - b_row), -jnp.inf)
        inter = b_col + m_ref[st:st + 1, 0:1]
        m_t = jnp.maximum(inter, jnp.max(dmat, axis=1, keepdims=True))
        sw_ref[st] = (s_ref[st] * jnp.exp2(dmat - m_t)).astype(BF16)
        mt_ref[st, 0] = jnp.broadcast_to(m_t, (L, LANES))
        mt_ref[st, 1] = jnp.broadcast_to(inter - m_t, (L, LANES))

    for st, (b, h) in enumerate(chains):
        hs = slice(h * dh, (h + 1) * dh)
        m_t = mt_ref[st, 0]
        inter_w = jnp.exp2(mt_ref[st, 1])
        qc = _dot(q_ref[b, :, hs], ct_ref[st].astype(BF16))
        nd = jnp.concatenate([inter_w, inter_w], axis=1) * qc + _dot(sw_ref[st], v_aug(b, h))
        hval = nd[:, :dh] / jnp.maximum(jnp.abs(nd[:, dh:]), jnp.exp2(-m_t))
        hn = _rms(hval, g_ref[:, hs])
        ym_ref[b, :, hs] = (hn * jax.nn.sigmoid(o_ref[b, :, hs].astype(F32))).astype(BF16)

    for st, (b, h) in enumerate(chains):
        hs = slice(h * dh, (h + 1) * dh)
        _, _, li_row, b_row, b_last = gate_terms(b, h)
        m_prev = m_ref[st:st + 1, 0:1]
        a_row = b_last - b_row + li_row
        m_new = jnp.maximum(b_last + m_prev, jnp.max(a_row, axis=1, keepdims=True))
        kw_t = (kt_ref[b, hs, :].astype(F32) * jnp.exp2(a_row - m_new)).astype(BF16)
        ct_ref[st] = jnp.exp2(b_last + m_prev - m_new) * ct_ref[st] + _dot(kw_t, v_aug(b, h))
        m_ref[st:st + 1, :] = jnp.broadcast_to(m_new, (1, LANES))


def _mlstm(qm, kmt, vm, om, zif, m_norm_g):
    bsz, seq, width = qm.shape
    L = MLSTM_CHUNK
    row = lambda c: (0, c, 0)
    rows = pl.BlockSpec((bsz, L, width), row)
    return pl.pallas_call(
        _mlstm_kernel,
        grid=(seq // L,),
        in_specs=[rows, pl.BlockSpec((bsz, width, L), lambda c: (0, 0, c)), rows, rows,
                  pl.BlockSpec((bsz, L, LANES), row), _const_spec((1, width))],
        out_specs=pl.BlockSpec((bsz, L, width), row),
        out_shape=jax.ShapeDtypeStruct((bsz, seq, width), BF16),
        scratch_shapes=[pltpu.VMEM((bsz * M_HEADS, M_HEAD_DIM, 2 * M_HEAD_DIM), F32),
                        pltpu.VMEM((bsz * M_HEADS, LANES), F32),
                        pltpu.VMEM((bsz * M_HEADS, L, L), F32),
                        pltpu.VMEM((bsz * M_HEADS, L, L), BF16),
                        pltpu.VMEM((bsz * M_HEADS, 2, L, LANES), F32)],
        compiler_params=_cparams(("arbitrary",)),
        name="mlstm",
    )(qm, kmt, vm, om, zif, m_norm_g)


def _moba_kernel(qt_ref, k_ref, vt_ref, kmean_ref, o_ref, qaug_ref, s_ref, p_ref, acc_ref):
    tq = blk = MOBA_BLOCK
    n_tiles = qt_ref.shape[2] // tq
    nb = k_ref.shape[1] // blk
    dh = A_HEAD_DIM
    feat = lax.broadcasted_iota(jnp.int32, (LANES, tq), 0)
    kmean = kmean_ref[0]
    blk_id = lax.broadcasted_iota(jnp.int32, (nb, tq), 0)
    lane_k = lax.broadcasted_iota(jnp.int32, (blk, LANES), 1)
    key_i = lax.broadcasted_iota(jnp.int32, (blk, 2 * tq), 0)
    qry_i = lax.broadcasted_iota(jnp.int32, (blk, 2 * tq), 1) & (tq - 1)
    ones_rows = jnp.ones((ACC_ROWS - dh, blk), BF16)

    def fold8(x, op):
        out = x[0:SUBLANES]
        for r in range(SUBLANES, x.shape[0], SUBLANES):
            out = op(out, x[r:r + SUBLANES])
        return out

    def make_tile(t):
        i = pl.program_id(2) * n_tiles + t
        qaug_t, s_t, p_t, acc_t = qaug_ref.at[t], s_ref.at[t], p_ref.at[t], acc_ref.at[t]

        def put_scores(j, slot):
            st = pl.multiple_of(j * blk, blk)
            onehot = jnp.where(lane_k == j, 1.0, 0.0).astype(BF16)
            k_aug = jnp.concatenate([k_ref[0, pl.ds(st, blk), :], onehot], axis=1)
            s_t[slot] = _dot(k_aug, qaug_t[...])
            return jnp.max(fold8(s_t[slot], jnp.maximum), axis=0, keepdims=True)

        def softmax_into(s_slot, p_slot, m_new):
            rows = 4 * SUBLANES
            for r in range(0, blk, rows):
                x = (s_t[s_slot, r:r + rows, :] - m_new).astype(BF16)
                p_t[p_slot, r:r + rows, :] = jnp.exp2(x)

        def add_pv(slot, vidx, alpha):
            st = pl.multiple_of(vidx * blk, blk)
            for hh in range(2):
                qs = slice(hh * tq, (hh + 1) * tq)
                v_aug = jnp.concatenate([vt_ref[0, hh * dh:(hh + 1) * dh, pl.ds(st, blk)], ones_rows], axis=0)
                acc_t[hh] = alpha[:, qs] * acc_t[hh] + _dot(v_aug, p_t[slot, :, qs])

        def prologue():
            qt = qt_ref[0, :, t * tq:(t + 1) * tq].astype(F32)
            valid = blk_id < i
            q_own, q_aug = [], []
            for hh in range(2):
                in_head = (feat >= hh * dh) & (feat < (hh + 1) * dh)
                qh = jnp.where(in_head, qt, 0.0).astype(BF16)
                g = jnp.where(valid, _dot(kmean, qh), -jnp.inf)
                sel = jnp.zeros(g.shape, jnp.bool_)
                for _ in range(MOBA_TOPK):
                    mx = jnp.max(g, axis=0, keepdims=True)
                    idx = jnp.min(jnp.where(g == mx, blk_id, nb), axis=0, keepdims=True)
                    pick = (blk_id == idx) & valid
                    sel = sel | pick
                    g = jnp.where(pick, -jnp.inf, g)
                bias = jnp.where(sel, 0.0, NEG_BIG).astype(BF16)
                q_own.append(qh)
                q_aug.append(jnp.concatenate([qh, bias, jnp.zeros((LANES - nb, tq), BF16)], axis=0))
            qaug_t[...] = jnp.concatenate(q_aug, axis=1)

            start = pl.multiple_of(i * blk, blk)
            s = _dot(k_ref[0, pl.ds(start, blk), :], jnp.concatenate(q_own, axis=1))
            own = MOBA_SCORE_SLOTS - 1
            s_t[own] = jnp.where(key_i <= qry_i, s, NEG_BIG)
            m0 = jnp.max(fold8(s_t[own], jnp.maximum), axis=0, keepdims=True)
            softmax_into(own, 1, m0)
            acc_t[...] = jnp.zeros(acc_t.shape, F32)
            cmax0 = put_scores(0, 0)
            return cmax0, jnp.ones_like(m0), i, m0

        def step(j, u, state):
            cmax, alpha_pend, vidx, m = state
            cmax_next = put_scores(jnp.minimum(j + 1, nb - 1), (u + 1) % MOBA_SCORE_SLOTS)
            add_pv((u + 1) % 2, vidx, alpha_pend)
            m_new = jnp.maximum(m, cmax)
            softmax_into(u % MOBA_SCORE_SLOTS, u % 2, m_new)
            return cmax_next, jnp.exp2(m - m_new), j, m_new

        def run(first_block, u0, n_trips, unroll, state):
            def body(trip, st):
                for u in range(unroll):
                    st = step(first_block + unroll * trip + u, u0 + u, st)
                return st
            return lax.fori_loop(0, n_trips, body, state)

        def loops(state):
            done = 0
            for unroll in MOBA_UNROLLS:
                trips = (i - done) // unroll
                state = run(done, 0, trips, unroll, state)
                done = done + trips * unroll
            pairs = (i - done + 1) // 2
            state = run(done, 0, jnp.minimum(pairs, 1), 2, state)
            return run(done + 2, 2, pairs - jnp.minimum(pairs, 1), 2, state)

        def epilogue(state):
            _, alpha_pend, vidx, _ = state
            add_pv(1, vidx, alpha_pend)
            out = jnp.concatenate([acc_t[hh, :dh, :] / acc_t[hh, dh:dh + 1, :] for hh in range(2)],
                                  axis=0)
            o_ref[0, t * tq:(t + 1) * tq, :] = out.T.astype(BF16)

        return prologue, loops, epilogue

    tiles = [make_tile(t) for t in range(n_tiles)]
    states = [prologue() for prologue, _, _ in tiles]
    states = [loops(state) for (_, loops, _), state in zip(tiles, states)]
    for (_, _, epilogue), state in zip(tiles, states):
        epilogue(state)


def _moba(qat, ka, vat, kmean):
    bsz, width, seq = qat.shape
    tq = MOBA_BLOCK
    tw = MOBA_TILES * tq
    npair = width // LANES
    return pl.pallas_call(
        _moba_kernel,
        grid=(bsz, npair, seq // tw),
        in_specs=[pl.BlockSpec((1, LANES, tw), lambda b, p, i: (b, p, i)),
                  pl.BlockSpec((1, seq, LANES), lambda b, p, i: (b, 0, p)),
                  pl.BlockSpec((1, LANES, seq), lambda b, p, i: (b, p, 0)),
                  pl.BlockSpec((1, kmean.shape[1], LANES), lambda b, p, i: (b, 0, p))],
        out_specs=pl.BlockSpec((1, tw, LANES), lambda b, p, i: (b, i, p)),
        out_shape=jax.ShapeDtypeStruct((bsz, seq, width), BF16),
        scratch_shapes=[pltpu.VMEM((MOBA_TILES, 2 * LANES, 2 * tq), BF16),
                        pltpu.VMEM((MOBA_TILES, MOBA_SCORE_SLOTS, MOBA_BLOCK, 2 * tq), F32),
                        pltpu.VMEM((MOBA_TILES, 2, MOBA_BLOCK, 2 * tq), BF16),
                        pltpu.VMEM((MOBA_TILES, 2, ACC_ROWS, tq), F32)],
        compiler_params=_cparams(("arbitrary", "arbitrary", "arbitrary")),
        name="moba",
    )(qat, ka, vat, kmean)


def _tail_kernel(x_ref, ym_ref, ya_ref, sga_ref, sgb_ref, mod_ref, g2_ref, gf_ref,
                 pm_ref, pa_ref, wo_ref, wg_ref, wu_ref, wd_ref, o_ref, *, final_norm):
    gate1 = mod_ref[0, 2:3, :]
    sh = mod_ref[0, 3:4, :]
    sc = mod_ref[0, 4:5, :]
    gate = mod_ref[0, 5:6, :]

    merged = (sga_ref[0].astype(F32) * _dot(ym_ref[0], pm_ref[...])
              + sgb_ref[0].astype(F32) * _dot(ya_ref[0], pa_ref[...]))
    x = x_ref[0] + gate1 * _dot(merged.astype(BF16), wo_ref[...])

    h = (_rms(x, g2_ref[...]) * (1.0 + sc) + sh).astype(BF16)
    d_ff = wg_ref.shape[1]
    step = 256
    acc = jnp.zeros(x.shape, F32)
    for c0 in range(0, d_ff, step):
        a = _dot(h, wg_ref[:, c0:c0 + step])
        u = _dot(h, wu_ref[:, c0:c0 + step])
        f = (a * jax.nn.sigmoid(a) * u).astype(BF16)
        acc = acc + _dot(f, wd_ref[c0:c0 + step, :])
    y = x + gate * acc
    o_ref[0] = _rms(y, gf_ref[...]) if final_norm else y


def _tail(x, ym, ya, sga, sgb, mod3, norm2_g, normf_g, weights, final_norm):
    bsz, seq, d = x.shape
    tm = ROW_TILE
    row = lambda b, i: (b, i, 0)
    return pl.pallas_call(
        functools.partial(_tail_kernel, final_norm=final_norm),
        grid=(bsz, seq // tm),
        in_specs=[pl.BlockSpec((1, tm, d), row),
                  pl.BlockSpec((1, tm, ym.shape[2]), row),
                  pl.BlockSpec((1, tm, ya.shape[2]), row),
                  pl.BlockSpec((1, tm, d), row),
                  pl.BlockSpec((1, tm, d), row),
                  pl.BlockSpec((1, N_MOD, d), lambda b, i: (b, 0, 0)),
                  _const_spec((1, d)), _const_spec((1, d)),
                  *[_const_spec(w.shape) for w in weights]],
        out_specs=pl.BlockSpec((1, tm, d), row),
        out_shape=jax.ShapeDtypeStruct((bsz, seq, d), F32),
        compiler_params=_cparams(("arbitrary", "arbitrary")),
        name="merge_ffn_out",
    )(x, ym, ya, sga, sgb, mod3, norm2_g, normf_g, *weights)


def _in_proj_params(w_in, b_in, layer):
    m_end = 4 * M_WIDTH
    if_end = m_end + 2 * M_HEADS
    n_pad = LANES - 2 * M_HEADS
    wt = jnp.swapaxes(w_in[layer], 0, 1)
    b = b_in[layer].reshape(1, -1)
    weights = (wt[:m_end].astype(BF16), wt[if_end:].astype(BF16),
               jnp.pad(wt[m_end:if_end], ((0, n_pad), (0, 0))).astype(BF16))
    biases = (b[:, :m_end], b[:, if_end:], jnp.pad(b[:, m_end:if_end], ((0, 0), (0, n_pad))))
    return weights, biases


def kernel(x, c, positions, ada_w, ada_b, norm1_g, norm2_g, normf_g, w_in, b_in, conv_w, conv_b,
           m_norm_g, p_mlstm, p_moba, w_out, w_gate, w_up, w_down):
    bsz, seq, d = x.shape
    depth = ada_w.shape[0]
    n_blocks = seq // MOBA_BLOCK
    cos_t, sin_t = _rope_tables(positions)
    for l in range(depth):
        mod3 = _ada_mod(c, ada_w, ada_b, l).reshape(bsz, N_MOD, d)
        weights, biases = _in_proj_params(w_in, b_in, l)
        (qm, km, vm, om, qa, ka, va, sga, sgb, zif, kmean_tiles) = _in_proj(
            x, mod3, norm1_g[l].reshape(1, d), weights, biases, conv_w[l],
            conv_b[l].reshape(1, -1), cos_t, sin_t)
        ym = _mlstm(qm, km, vm, om, zif, m_norm_g[l].reshape(1, -1))
        kmean = kmean_tiles[:, :IN_ROW_TILE // MOBA_BLOCK].reshape(bsz, n_blocks, A_WIDTH)
        ya = _moba(qa, ka, va, kmean.astype(BF16))
        tail_weights = [w[l].astype(BF16) for w in (p_mlstm, p_moba, w_out, w_gate, w_up, w_down)]
        x = _tail(x, ym, ya, sga, sgb, mod3, norm2_g[l].reshape(1, d), normf_g.reshape(1, d),
                  tail_weights, final_norm=(l == depth - 1))
    return x
```

```python
import functools

import jax
import jax.numpy as jnp
from jax import lax
from jax.experimental import pallas as pl
from jax.experimental.pallas import tpu as pltpu

F32 = jnp.float32
BF16 = jnp.bfloat16

M_HEADS = 4
M_HEAD_DIM = 128
M_WIDTH = M_HEADS * M_HEAD_DIM
M_CONV = 4
A_HEADS = 8
A_HEAD_DIM = 64
A_WIDTH = A_HEADS * A_HEAD_DIM
MOBA_BLOCK = 256
MOBA_TOPK = 3
ROPE_THETA = 10000.0
N_MOD = 6
RMS_EPS = 1e-6

LANES = 128
SUBLANES = 8
VMEM_LIMIT = 56 * 1024 * 1024

ROW_TILE = 512
IN_ROW_TILE = 512
MLSTM_CHUNK = 512
NEG_BIG = -1e30
LOG2_E = 1.4426950408889634
MOBA_TILES = 8
MOBA_SCORE_SLOTS = 4
MOBA_UNROLLS = (16, 8, 4)
ACC_ROWS = A_HEAD_DIM + 16

OFF_MQ, OFF_MK, OFF_MV, OFF_MO = 0, 512, 1024, 1536
OFF_AQ, OFF_AK, OFF_AV = 2048, 2560, 3072
OFF_GA, OFF_GB = 3584, 4608
OFF_IF = 5632
D_IN_PACKED = 5760


def _cparams(sem, flags=None):
    return pltpu.CompilerParams(dimension_semantics=sem, vmem_limit_bytes=VMEM_LIMIT, flags=flags)


def _const_spec(shape):
    nd = len(shape)
    return pl.BlockSpec(shape, lambda *_: (0,) * nd, pipeline_mode=pl.Buffered(1))


def _rms(x, g):
    ms = jnp.mean(x * x, axis=-1, keepdims=True)
    return x * lax.rsqrt(ms + RMS_EPS) * g


def _dot(a, b):
    return jnp.dot(a, b, preferred_element_type=F32)


def _dot_nt(a, b):
    return lax.dot_general(a, b, (((1,), (1,)), ((), ())), preferred_element_type=F32)


def _ada_kernel(c_ref, w_ref, b_ref, o_ref):
    c = c_ref[...]
    ca = c * jax.nn.sigmoid(c)
    o_ref[...] = jnp.dot(ca, w_ref[...], precision=lax.Precision.HIGHEST,
                         preferred_element_type=F32) + b_ref[...]


def _ada_mod(c, ada_w, ada_b, layer):
    bsz, d = c.shape
    depth, _, n = ada_w.shape
    tn = d
    return pl.pallas_call(
        _ada_kernel,
        grid=(n // tn,),
        in_specs=[pl.BlockSpec((bsz, d), lambda j: (0, 0)),
                  pl.BlockSpec((None, d, tn), lambda j: (layer, 0, j)),
                  pl.BlockSpec((None, 1, tn), lambda j: (layer, 0, j))],
        out_specs=pl.BlockSpec((bsz, tn), lambda j: (0, j)),
        out_shape=jax.ShapeDtypeStruct((bsz, n), F32),
        compiler_params=_cparams(("arbitrary",)),
        name="ada_mod",
    )(c, ada_w, ada_b.reshape(depth, 1, n))


def _rope_kernel(pos_ref, freq_ref, cos_ref, sin_ref):
    ang = freq_ref[...] * pos_ref[0].astype(F32)
    c = jnp.cos(ang)
    s = jnp.sin(ang)
    cos_ref[0] = jnp.concatenate([c, c, c, c], axis=0).T
    sin_ref[0] = jnp.concatenate([-s, s, -s, s], axis=0).T


def _rope_tables(positions):
    bsz, seq = positions.shape
    half = A_HEAD_DIM // 2
    inv_freq = (ROPE_THETA ** (-jnp.arange(half, dtype=F32) / half)).reshape(half, 1)
    ts = 2048
    out = jax.ShapeDtypeStruct((bsz, seq, LANES), F32)
    return pl.pallas_call(
        _rope_kernel,
        grid=(bsz, seq // ts),
        in_specs=[pl.BlockSpec((1, 1, ts), lambda b, i: (b, 0, i)),
                  pl.BlockSpec((half, 1), lambda b, i: (0, 0))],
        out_specs=[pl.BlockSpec((1, ts, LANES), lambda b, i: (b, i, 0))] * 2,
        out_shape=[out, out],
        compiler_params=_cparams(("arbitrary", "arbitrary")),
        name="rope_table",
    )(positions.reshape(bsz, 1, seq), inv_freq)


def _in_proj_kernel(x_ref, mod_ref, g_ref, wm_ref, wa_ref, wif_ref, bm_ref, ba_ref, bif_ref,
                    cw_ref, cb_ref, cos_ref, sin_ref,
                    qm_ref, km_ref, vm_ref, om_ref, qa_ref, ka_ref, va_ref,
                    sga_ref, sgb_ref, zif_ref, kmean_ref, hb_ref, zbuf_ref):
    tm = x_ref.shape[1]
    it = pl.program_id(1)

    x = x_ref[0]
    sh = mod_ref[0, 0:1, :]
    sc = mod_ref[0, 1:2, :]
    hb_ref[...] = (_rms(x, g_ref[...]) * (1.0 + sc) + sh).astype(BF16)

    def proj(off, width):
        for base, w_ref, b_ref in ((OFF_IF, wif_ref, bif_ref), (OFF_AQ, wa_ref, ba_ref), (OFF_MQ, wm_ref, bm_ref)):
            if off >= base:
                lo = off - base
                return _dot_nt(hb_ref[...], w_ref[lo:lo + width, :]) + b_ref[:, lo:lo + width]

    @pl.when(it == 0)
    def _():
        zbuf_ref[0:SUBLANES, :] = jnp.zeros((SUBLANES, zbuf_ref.shape[1]), F32)

    @pl.when(it > 0)
    def _():
        zbuf_ref[0:SUBLANES, :] = zbuf_ref[tm:tm + SUBLANES, :]

    zbuf_ref[SUBLANES:SUBLANES + tm, 0:M_WIDTH] = proj(OFF_MQ, M_WIDTH)
    zbuf_ref[SUBLANES:SUBLANES + tm, M_WIDTH:2 * M_WIDTH] = proj(OFF_MK, M_WIDTH)

    def conv_silu(part):
        width = 2 * LANES
        cols = slice(part * width, (part + 1) * width)
        acc = jnp.broadcast_to(cb_ref[:, cols], (tm, width))
        for j in range(M_CONV):
            start = SUBLANES - (M_CONV - 1) + j
            acc = acc + zbuf_ref[start:start + tm, cols] * cw_ref[j:j + 1, cols]
        y = acc * jax.nn.sigmoid(acc)
        if part * width < M_WIDTH:
            qm_ref[0, :, cols] = y.astype(BF16)
        else:
            krows = slice(part * width - M_WIDTH, (part + 1) * width - M_WIDTH)
            km_ref[0, krows, :] = (y * M_HEAD_DIM ** -0.5).T.astype(BF16)

    conv_silu(0)
    vm_ref[0] = proj(OFF_MV, M_WIDTH).astype(BF16)
    conv_silu(1)
    om_ref[0] = proj(OFF_MO, M_WIDTH).astype(BF16)
    conv_silu(2)

    def rope(z):
        outs = []
        for p in range(A_WIDTH // LANES):
            zp = z[:, p * LANES:(p + 1) * LANES]
            lane = lax.broadcasted_iota(jnp.int32, zp.shape, 1)
            first_half = (lane % A_HEAD_DIM) < (A_HEAD_DIM // 2)
            rot = jnp.where(first_half,
                            pltpu.roll(zp, LANES - A_HEAD_DIM // 2, 1),
                            pltpu.roll(zp, A_HEAD_DIM // 2, 1))
            outs.append(zp * cos_ref[0] + rot * sin_ref[0])
        return outs

    q_parts = rope(proj(OFF_AQ, A_WIDTH))
    for p, qp in enumerate(q_parts):
        qa_ref[0, p * LANES:(p + 1) * LANES, :] = (qp * (LOG2_E * A_HEAD_DIM ** -0.5)).T.astype(BF16)
    conv_silu(3)

    k_parts = rope(proj(OFF_AK, A_WIDTH))
    kmean_ref[0] = jnp.zeros(kmean_ref.shape[1:], F32)
    for p, kp in enumerate(k_parts):
        ka_ref[0, :, p * LANES:(p + 1) * LANES] = kp.astype(BF16)
        for blk in range(tm // MOBA_BLOCK):
            kmean_ref[0, blk:blk + 1, p * LANES:(p + 1) * LANES] = jnp.mean(
                kp[blk * MOBA_BLOCK:(blk + 1) * MOBA_BLOCK], axis=0, keepdims=True)

    va = proj(OFF_AV, A_WIDTH)
    for p in range(A_WIDTH // LANES):
        va_ref[0, p * LANES:(p + 1) * LANES, :] = va[:, p * LANES:(p + 1) * LANES].T.astype(BF16)

    d = x.shape[1]
    for off, out_ref in ((OFF_GA, sga_ref), (OFF_GB, sgb_ref)):
        for c0 in range(0, d, 512):
            out_ref[0, :, c0:c0 + 512] = jax.nn.sigmoid(proj(off + c0, 512)).astype(BF16)

    zif_ref[0] = proj(OFF_IF, LANES)


def _in_proj(x, mod3, norm_g, weights, biases, conv_w, conv_b, cos_t, sin_t):
    bsz, seq, d = x.shape
    tm = IN_ROW_TILE
    nt = seq // tm
    row = lambda b, i: (b, i, 0)

    def rows(width, dtype):
        return (pl.BlockSpec((1, tm, width), row), jax.ShapeDtypeStruct((bsz, seq, width), dtype))

    cols = (pl.BlockSpec((1, A_WIDTH, tm), lambda b, i: (b, 0, i)),
            jax.ShapeDtypeStruct((bsz, A_WIDTH, seq), BF16))
    outs = ([rows(M_WIDTH, BF16), cols] + [rows(M_WIDTH, BF16)] * 2 + [cols, rows(A_WIDTH, BF16), cols]
            + [rows(d, BF16)] * 2 + [rows(LANES, F32)])
    outs.append((pl.BlockSpec((1, SUBLANES, A_WIDTH), lambda b, i: (b * nt + i, 0, 0)),
                 jax.ShapeDtypeStruct((bsz * nt, SUBLANES, A_WIDTH), F32)))
    return pl.pallas_call(
        _in_proj_kernel,
        grid=(bsz, nt),
        in_specs=[pl.BlockSpec((1, tm, d), row),
                  pl.BlockSpec((1, N_MOD, d), lambda b, i: (b, 0, 0)),
                  _const_spec((1, d)),
                  *[_const_spec(t.shape) for t in (*weights, *biases)],
                  _const_spec(conv_w.shape),
                  _const_spec(conv_b.shape),
                  pl.BlockSpec((1, tm, LANES), row),
                  pl.BlockSpec((1, tm, LANES), row)],
        out_specs=[o[0] for o in outs],
        out_shape=[o[1] for o in outs],
        scratch_shapes=[pltpu.VMEM((tm, d), BF16),
                        pltpu.VMEM((tm + SUBLANES, 2 * M_WIDTH), F32)],
        compiler_params=_cparams(("arbitrary", "arbitrary")),
        name="in_proj",
    )(x, mod3, norm_g, *weights, *biases, conv_w, conv_b, cos_t, sin_t)


def _split3(x):
    hi = x.astype(BF16)
    r1 = x - hi.astype(F32)
    mid = r1.astype(BF16)
    lo = (r1 - mid.astype(F32)).astype(BF16)
    return hi, mid, lo


def _mlstm_kernel(q_ref, kt_ref, v_ref, o_ref, zif_ref, g_ref, ym_ref, ct_ref, m_ref, s_ref, sw_ref, mt_ref):
    bsz, L = q_ref.shape[0], q_ref.shape[1]
    dh = M_HEAD_DIM

    @pl.when(pl.program_id(0) == 0)
    def _():
        ct_ref[...] = jnp.zeros(ct_ref.shape, F32)
        m_ref[...] = jnp.zeros(m_ref.shape, F32)

    r_i = lax.broadcasted_iota(jnp.int32, (L, L), 0)
    c_i = lax.broadcasted_iota(jnp.int32, (L, L), 1)
    causal = c_i <= r_i
    triu = jnp.where(r_i <= c_i, 1.0, 0.0).astype(BF16)
    ones = jnp.ones((L, dh), BF16)
    gate_row = lax.broadcasted_iota(jnp.int32, (SUBLANES, L), 0)

    gate_logs = []
    for b in range(bsz):
        gates = zif_ref[b].T[0:SUBLANES, :]
        log_f = jnp.minimum(gates, 0.0) - jnp.log1p(jnp.exp(-jnp.abs(gates)))
        log_f = jnp.where(gate_row >= M_HEADS, log_f, 0.0)
        bcum = sum(_dot(part, triu) for part in _split3(log_f))
        xrow = jnp.where(gate_row < M_HEADS, gates, bcum) * LOG2_E
        xcol = jnp.concatenate([xrow, jnp.zeros((LANES - SUBLANES, L), F32)], axis=0).T
        gate_logs.append((xrow, xcol))

    chains = [(b, h) for b in range(bsz) for h in range(M_HEADS)]

    def gate_terms(b, h):
        xrow, xcol = gate_logs[b]
        li_col = xcol[:, h:h + 1]
        b_col = xcol[:, M_HEADS + h:M_HEADS + h + 1]
        li_row = xrow[h:h + 1, :]
        b_row = xrow[M_HEADS + h:M_HEADS + h + 1, :]
        return li_col, b_col, li_row, b_row, b_col[L - 1:L, :]

    def v_aug(b, h):
        return jnp.concatenate([v_ref[b, :, h * dh:(h + 1) * dh], ones], axis=1)

    for st, (b, h) in enumerate(chains):
        hs = slice(h * dh, (h + 1) * dh)
        s_ref[st] = _dot(q_ref[b, :, hs], kt_ref[b, hs, :])

    for st, (b, h) in enumerate(chains):
        _, b_col, li_row, b_row, _ = gate_terms(b, h)
        dmat = jnp.where(causal, b_col + (li_row - b_row), -jnp.inf)
        inter = b_col + m_ref[st:st + 1, 0:1]
        m_t = jnp.maximum(inter, jnp.max(dmat, axis=1, keepdims=True))
        sw_ref[st] = (s_ref[st] * jnp.exp2(dmat - m_t)).astype(BF16)
        mt_ref[st, 0] = jnp.broadcast_to(m_t, (L, LANES))
        mt_ref[st, 1] = jnp.broadcast_to(inter - m_t, (L, LANES))

    for st, (b, h) in enumerate(chains):
        hs = slice(h * dh, (h + 1) * dh)
        m_t = mt_ref[st, 0]
        inter_w = jnp.exp2(mt_ref[st, 1])
        qc = _dot(q_ref[b, :, hs], ct_ref[st].astype(BF16))
        nd = jnp.concatenate([inter_w, inter_w], axis=1) * qc + _dot(sw_ref[st], v_aug(b, h))
        hval = nd[:, :dh] / jnp.maximum(jnp.abs(nd[:, dh:]), jnp.exp2(-m_t))
        hn = _rms(hval, g_ref[:, hs])
        ym_ref[b, :, hs] = (hn * jax.nn.sigmoid(o_ref[b, :, hs].astype(F32))).astype(BF16)

    for st, (b, h) in enumerate(chains):
        hs = slice(h * dh, (h + 1) * dh)
        _, _, li_row, b_row, b_last = gate_terms(b, h)
        m_prev = m_ref[st:st + 1, 0:1]
        a_row = b_last - b_row + li_row
        m_new = jnp.maximum(b_last + m_prev, jnp.max(a_row, axis=1, keepdims=True))
        kw_t = (kt_ref[b, hs, :].astype(F32) * jnp.exp2(a_row - m_new)).astype(BF16)
        ct_ref[st] = jnp.exp2(b_last + m_prev - m_new) * ct_ref[st] + _dot(kw_t, v_aug(b, h))
        m_ref[st:st + 1, :] = jnp.broadcast_to(m_new, (1, LANES))


def _mlstm(qm, kmt, vm, om, zif, m_norm_g):
    bsz, seq, width = qm.shape
    L = MLSTM_CHUNK
    row = lambda c: (0, c, 0)
    rows = pl.BlockSpec((bsz, L, width), row)
    return pl.pallas_call(
        _mlstm_kernel,
        grid=(seq // L,),
        in_specs=[rows, pl.BlockSpec((bsz, width, L), lambda c: (0, 0, c)), rows, rows,
                  pl.BlockSpec((bsz, L, LANES), row), _const_spec((1, width))],
        out_specs=pl.BlockSpec((bsz, L, width), row),
        out_shape=jax.ShapeDtypeStruct((bsz, seq, width), BF16),
        scratch_shapes=[pltpu.VMEM((bsz * M_HEADS, M_HEAD_DIM, 2 * M_HEAD_DIM), F32),
                        pltpu.VMEM((bsz * M_HEADS, LANES), F32),
                        pltpu.VMEM((bsz * M_HEADS, L, L), F32),
                        pltpu.VMEM((bsz * M_HEADS, L, L), BF16),
                        pltpu.VMEM((bsz * M_HEADS, 2, L, LANES), F32)],
        compiler_params=_cparams(("arbitrary",)),
        name="mlstm",
    )(qm, kmt, vm, om, zif, m_norm_g)


def _moba_kernel(qt_ref, k_ref, vt_ref, kmean_ref, o_ref, qaug_ref, s_ref, p_ref, acc_ref):
    tq = blk = MOBA_BLOCK
    n_tiles = qt_ref.shape[2] // tq
    nb = k_ref.shape[1] // blk
    dh = A_HEAD_DIM
    feat = lax.broadcasted_iota(jnp.int32, (LANES, tq), 0)
    kmean = kmean_ref[0]
    blk_id = lax.broadcasted_iota(jnp.int32, (nb, tq), 0)
    lane_k = lax.broadcasted_iota(jnp.int32, (blk, LANES), 1)
    key_i = lax.broadcasted_iota(jnp.int32, (blk, 2 * tq), 0)
    qry_i = lax.broadcasted_iota(jnp.int32, (blk, 2 * tq), 1) & (tq - 1)
    ones_rows = jnp.ones((ACC_ROWS - dh, blk), BF16)

    def fold8(x, op):
        out = x[0:SUBLANES]
        for r in range(SUBLANES, x.shape[0], SUBLANES):
            out = op(out, x[r:r + SUBLANES])
        return out

    def make_tile(t):
        i = pl.program_id(2) * n_tiles + t
        qaug_t, s_t, p_t, acc_t = qaug_ref.at[t], s_ref.at[t], p_ref.at[t], acc_ref.at[t]

        def put_scores(j, slot):
            st = pl.multiple_of(j * blk, blk)
            onehot = jnp.where(lane_k == j, 1.0, 0.0).astype(BF16)
            k_aug = jnp.concatenate([k_ref[0, pl.ds(st, blk), :], onehot], axis=1)
            s_t[slot] = _dot(k_aug, qaug_t[...])
            return jnp.max(fold8(s_t[slot], jnp.maximum), axis=0, keepdims=True)

        def softmax_into(s_slot, p_slot, m_new):
            rows = 4 * SUBLANES
            for r in range(0, blk, rows):
                x = (s_t[s_slot, r:r + rows, :] - m_new).astype(BF16)
                p_t[p_slot, r:r + rows, :] = jnp.exp2(x)

        def add_pv(slot, vidx, alpha):
            st = pl.multiple_of(vidx * blk, blk)
            for hh in range(2):
                qs = slice(hh * tq, (hh + 1) * tq)
                v_aug = jnp.concatenate([vt_ref[0, hh * dh:(hh + 1) * dh, pl.ds(st, blk)], ones_rows], axis=0)
                acc_t[hh] = alpha[:, qs] * acc_t[hh] + _dot(v_aug, p_t[slot, :, qs])

        def prologue():
            qt = qt_ref[0, :, t * tq:(t + 1) * tq].astype(F32)
            valid = blk_id < i
            q_own, q_aug = [], []
            for hh in range(2):
                in_head = (feat >= hh * dh) & (feat < (hh + 1) * dh)
                qh = jnp.where(in_head, qt, 0.0).astype(BF16)
                g = jnp.where(valid, _dot(kmean, qh), -jnp.inf)
                sel = jnp.zeros(g.shape, jnp.bool_)
                for _ in range(MOBA_TOPK):
                    mx = jnp.max(g, axis=0, keepdims=True)
                    idx = jnp.min(jnp.where(g == mx, blk_id, nb), axis=0, keepdims=True)
                    pick = (blk_id == idx) & valid
                    sel = sel | pick
                    g = jnp.where(pick, -jnp.inf, g)
                bias = jnp.where(sel, 0.0, NEG_BIG).astype(BF16)
                q_own.append(qh)
                q_aug.append(jnp.concatenate([qh, bias, jnp.zeros((LANES - nb, tq), BF16)], axis=0))
            qaug_t[...] = jnp.concatenate(q_aug, axis=1)

            start = pl.multiple_of(i * blk, blk)
            s = _dot(k_ref[0, pl.ds(start, blk), :], jnp.concatenate(q_own, axis=1))
            own = MOBA_SCORE_SLOTS - 1
            s_t[own] = jnp.where(key_i <= qry_i, s, NEG_BIG)
            m0 = jnp.max(fold8(s_t[own], jnp.maximum), axis=0, keepdims=True)
            softmax_into(own, 1, m0)
            acc_t[...] = jnp.zeros(acc_t.shape, F32)
            cmax0 = put_scores(0, 0)
            return cmax0, jnp.ones_like(m0), i, m0

        def step(j, u, state):
            cmax, alpha_pend, vidx, m = state
            cmax_next = put_scores(jnp.minimum(j + 1, nb - 1), (u + 1) % MOBA_SCORE_SLOTS)
            add_pv((u + 1) % 2, vidx, alpha_pend)
            m_new = jnp.maximum(m, cmax)
            softmax_into(u % MOBA_SCORE_SLOTS, u % 2, m_new)
            return cmax_next, jnp.exp2(m - m_new), j, m_new

        def run(first_block, u0, n_trips, unroll, state):
            def body(trip, st):
                for u in range(unroll):
                    st = step(first_block + unroll * trip + u, u0 + u, st)
                return st
            return lax.fori_loop(0, n_trips, body, state)

        def loops(state):
            done = 0
            for unroll in MOBA_UNROLLS:
                trips = (i - done) // unroll
                state = run(done, 0, trips, unroll, state)
                done = done + trips * unroll
            pairs = (i - done + 1) // 2
            state = run(done, 0, jnp.minimum(pairs, 1), 2, state)
            return run(done + 2, 2, pairs - jnp.minimum(pairs, 1), 2, state)

        def epilogue(state):
            _, alpha_pend, vidx, _ = state
            add_pv(1, vidx, alpha_pend)
            out = jnp.concatenate([acc_t[hh, :dh, :] / acc_t[hh, dh:dh + 1, :] for hh in range(2)],
                                  axis=0)
            o_ref[0, t * tq:(t + 1) * tq, :] = out.T.astype(BF16)

        return prologue, loops, epilogue

    tiles = [make_tile(t) for t in range(n_tiles)]
    states = [prologue() for prologue, _, _ in tiles]
    states = [loops(state) for (_, loops, _), state in zip(tiles, states)]
    for (_, _, epilogue), state in zip(tiles, states):
        epilogue(state)


def _moba(qat, ka, vat, kmean):
    bsz, width, seq = qat.shape
    tq = MOBA_BLOCK
    tw = MOBA_TILES * tq
    npair = width // LANES
    return pl.pallas_call(
        _moba_kernel,
        grid=(bsz, npair, seq // tw),
        in_specs=[pl.BlockSpec((1, LANES, tw), lambda b, p, i: (b, p, i)),
                  pl.BlockSpec((1, seq, LANES), lambda b, p, i: (b, 0, p)),
                  pl.BlockSpec((1, LANES, seq), lambda b, p, i: (b, p, 0)),
                  pl.BlockSpec((1, kmean.shape[1], LANES), lambda b, p, i: (b, 0, p))],
        out_specs=pl.BlockSpec((1, tw, LANES), lambda b, p, i: (b, i, p)),
        out_shape=jax.ShapeDtypeStruct((bsz, seq, width), BF16),
        scratch_shapes=[pltpu.VMEM((MOBA_TILES, 2 * LANES, 2 * tq), BF16),
                        pltpu.VMEM((MOBA_TILES, MOBA_SCORE_SLOTS, MOBA_BLOCK, 2 * tq), F32),
                        pltpu.VMEM((MOBA_TILES, 2, MOBA_BLOCK, 2 * tq), BF16),
                        pltpu.VMEM((MOBA_TILES, 2, ACC_ROWS, tq), F32)],
        compiler_params=_cparams(("arbitrary", "arbitrary", "arbitrary")),
        name="moba",
    )(qat, ka, vat, kmean)


def _tail_kernel(x_ref, ym_ref, ya_ref, sga_ref, sgb_ref, mod_ref, g2_ref, gf_ref,
                 pm_ref, pa_ref, wo_ref, wg_ref, wu_ref, wd_ref, o_ref, *, final_norm):
    gate1 = mod_ref[0, 2:3, :]
    sh = mod_ref[0, 3:4, :]
    sc = mod_ref[0, 4:5, :]
    gate = mod_ref[0, 5:6, :]

    merged = (sga_ref[0].astype(F32) * _dot(ym_ref[0], pm_ref[...])
              + sgb_ref[0].astype(F32) * _dot(ya_ref[0], pa_ref[...]))
    x = x_ref[0] + gate1 * _dot(merged.astype(BF16), wo_ref[...])

    h = (_rms(x, g2_ref[...]) * (1.0 + sc) + sh).astype(BF16)
    d_ff = wg_ref.shape[1]
    step = 256
    acc = jnp.zeros(x.shape, F32)
    for c0 in range(0, d_ff, step):
        a = _dot(h, wg_ref[:, c0:c0 + step])
        u = _dot(h, wu_ref[:, c0:c0 + step])
        f = (a * jax.nn.sigmoid(a) * u).astype(BF16)
        acc = acc + _dot(f, wd_ref[c0:c0 + step, :])
    y = x + gate * acc
    o_ref[0] = _rms(y, gf_ref[...]) if final_norm else y


def _tail(x, ym, ya, sga, sgb, mod3, norm2_g, normf_g, weights, final_norm):
    bsz, seq, d = x.shape
    tm = ROW_TILE
    row = lambda b, i: (b, i, 0)
    return pl.pallas_call(
        functools.partial(_tail_kernel, final_norm=final_norm),
        grid=(bsz, seq // tm),
        in_specs=[pl.BlockSpec((1, tm, d), row),
                  pl.BlockSpec((1, tm, ym.shape[2]), row),
                  pl.BlockSpec((1, tm, ya.shape[2]), row),
                  pl.BlockSpec((1, tm, d), row),
                  pl.BlockSpec((1, tm, d), row),
                  pl.BlockSpec((1, N_MOD, d), lambda b, i: (b, 0, 0)),
                  _const_spec((1, d)), _const_spec((1, d)),
                  *[_const_spec(w.shape) for w in weights]],
        out_specs=pl.BlockSpec((1, tm, d), row),
        out_shape=jax.ShapeDtypeStruct((bsz, seq, d), F32),
        compiler_params=_cparams(("arbitrary", "arbitrary")),
        name="merge_ffn_out",
    )(x, ym, ya, sga, sgb, mod3, norm2_g, normf_g, *weights)


def _in_proj_params(w_in, b_in, layer):
    m_end = 4 * M_WIDTH
    if_end = m_end + 2 * M_HEADS
    n_pad = LANES - 2 * M_HEADS
    wt = jnp.swapaxes(w_in[layer], 0, 1)
    b = b_in[layer].reshape(1, -1)
    weights = (wt[:m_end].astype(BF16), wt[if_end:].astype(BF16),
               jnp.pad(wt[m_end:if_end], ((0, n_pad), (0, 0))).astype(BF16))
    biases = (b[:, :m_end], b[:, if_end:], jnp.pad(b[:, m_end:if_end], ((0, 0), (0, n_pad))))
    return weights, biases


def kernel(x, c, positions, ada_w, ada_b, norm1_g, norm2_g, normf_g, w_in, b_in, conv_w, conv_b,
           m_norm_g, p_mlstm, p_moba, w_out, w_gate, w_up, w_down):
    bsz, seq, d = x.shape
    depth = ada_w.shape[0]
    n_blocks = seq // MOBA_BLOCK
    cos_t, sin_t = _rope_tables(positions)
    for l in range(depth):
        mod3 = _ada_mod(c, ada_w, ada_b, l).reshape(bsz, N_MOD, d)
        weights, biases = _in_proj_params(w_in, b_in, l)
        (qm, km, vm, om, qa, ka, va, sga, sgb, zif, kmean_tiles) = _in_proj(
            x, mod3, norm1_g[l].reshape(1, d), weights, biases, conv_w[l],
            conv_b[l].reshape(1, -1), cos_t, sin_t)
        ym = _mlstm(qm, km, vm, om, zif, m_norm_g[l].reshape(1, -1))
        kmean = kmean_tiles[:, :IN_ROW_TILE // MOBA_BLOCK].reshape(bsz, n_blocks, A_WIDTH)
        ya = _moba(qa, ka, va, kmean.astype(BF16))
        tail_weights = [w[l].astype(BF16) for w in (p_mlstm, p_moba, w_out, w_gate, w_up, w_down)]
        x = _tail(x, ym, ya, sga, sgb, mod3, norm2_g[l].reshape(1, d), normf_g.reshape(1, d),
                  tail_weights, final_norm=(l == depth - 1))
    return x
```

```python
import functools

import jax
import jax.numpy as jnp
from jax import lax
from jax.experimental import pallas as pl
from jax.experimental.pallas import tpu as pltpu

F32 = jnp.float32
BF16 = jnp.bfloat16

M_HEADS = 4
M_HEAD_DIM = 128
M_WIDTH = M_HEADS * M_HEAD_DIM
M_CONV = 4
A_HEADS = 8
A_HEAD_DIM = 64
A_WIDTH = A_HEADS * A_HEAD_DIM
MOBA_BLOCK = 256
MOBA_TOPK = 3
ROPE_THETA = 10000.0
N_MOD = 6
RMS_EPS = 1e-6

LANES = 128
SUBLANES = 8
VMEM_LIMIT = 56 * 1024 * 1024

ROW_TILE = 512
IN_ROW_TILE = 512
MLSTM_CHUNK = 512
NEG_BIG = -1e30
LOG2_E = 1.4426950408889634
MOBA_TILES = 8
MOBA_SCORE_SLOTS = 4
MOBA_UNROLLS = (8, 4)
ACC_ROWS = A_HEAD_DIM + 16

OFF_MQ, OFF_MK, OFF_MV, OFF_MO = 0, 512, 1024, 1536
OFF_AQ, OFF_AK, OFF_AV = 2048, 2560, 3072
OFF_GA, OFF_GB = 3584, 4608
OFF_IF = 5632
D_IN_PACKED = 5760


def _cparams(sem, flags=None):
    return pltpu.CompilerParams(dimension_semantics=sem, vmem_limit_bytes=VMEM_LIMIT, flags=flags)


def _const_spec(shape):
    nd = len(shape)
    return pl.BlockSpec(shape, lambda *_: (0,) * nd, pipeline_mode=pl.Buffered(1))


def _rms(x, g):
    ms = jnp.mean(x * x, axis=-1, keepdims=True)
    return x * lax.rsqrt(ms + RMS_EPS) * g


def _dot(a, b):
    return jnp.dot(a, b, preferred_element_type=F32)


def _dot_nt(a, b):
    return lax.dot_general(a, b, (((1,), (1,)), ((), ())), preferred_element_type=F32)


def _ada_kernel(c_ref, w_ref, b_ref, o_ref):
    c = c_ref[...]
    ca = c * jax.nn.sigmoid(c)
    o_ref[...] = jnp.dot(ca, w_ref[...], precision=lax.Precision.HIGHEST,
                         preferred_element_type=F32) + b_ref[...]


def _ada_mod(c, ada_w, ada_b, layer):
    bsz, d = c.shape
    depth, _, n = ada_w.shape
    tn = d
    return pl.pallas_call(
        _ada_kernel,
        grid=(n // tn,),
        in_specs=[pl.BlockSpec((bsz, d), lambda j: (0, 0)),
                  pl.BlockSpec((None, d, tn), lambda j: (layer, 0, j)),
                  pl.BlockSpec((None, 1, tn), lambda j: (layer, 0, j))],
        out_specs=pl.BlockSpec((bsz, tn), lambda j: (0, j)),
        out_shape=jax.ShapeDtypeStruct((bsz, n), F32),
        compiler_params=_cparams(("arbitrary",)),
        name="ada_mod",
    )(c, ada_w, ada_b.reshape(depth, 1, n))


def _rope_kernel(pos_ref, freq_ref, cos_ref, sin_ref):
    ang = freq_ref[...] * pos_ref[0].astype(F32)
    c = jnp.cos(ang)
    s = jnp.sin(ang)
    cos_ref[0] = jnp.concatenate([c, c, c, c], axis=0).T
    sin_ref[0] = jnp.concatenate([-s, s, -s, s], axis=0).T


def _rope_tables(positions):
    bsz, seq = positions.shape
    half = A_HEAD_DIM // 2
    inv_freq = (ROPE_THETA ** (-jnp.arange(half, dtype=F32) / half)).reshape(half, 1)
    ts = 2048
    out = jax.ShapeDtypeStruct((bsz, seq, LANES), F32)
    return pl.pallas_call(
        _rope_kernel,
        grid=(bsz, seq // ts),
        in_specs=[pl.BlockSpec((1, 1, ts), lambda b, i: (b, 0, i)),
                  pl.BlockSpec((half, 1), lambda b, i: (0, 0))],
        out_specs=[pl.BlockSpec((1, ts, LANES), lambda b, i: (b, i, 0))] * 2,
        out_shape=[out, out],
        compiler_params=_cparams(("arbitrary", "arbitrary")),
        name="rope_table",
    )(positions.reshape(bsz, 1, seq), inv_freq)


def _in_proj_kernel(x_ref, mod_ref, g_ref, wm_ref, wa_ref, wif_ref, bm_ref, ba_ref, bif_ref,
                    cw_ref, cb_ref, cos_ref, sin_ref,
                    qm_ref, km_ref, vm_ref, om_ref, qa_ref, ka_ref, va_ref,
                    sga_ref, sgb_ref, zif_ref, kmean_ref, hb_ref, zbuf_ref):
    tm = x_ref.shape[1]
    it = pl.program_id(1)

    x = x_ref[0]
    sh = mod_ref[0, 0:1, :]
    sc = mod_ref[0, 1:2, :]
    hb_ref[...] = (_rms(x, g_ref[...]) * (1.0 + sc) + sh).astype(BF16)

    def proj(off, width):
        for base, w_ref, b_ref in ((OFF_IF, wif_ref, bif_ref), (OFF_AQ, wa_ref, ba_ref), (OFF_MQ, wm_ref, bm_ref)):
            if off >= base:
                lo = off - base
                return _dot_nt(hb_ref[...], w_ref[lo:lo + width, :]) + b_ref[:, lo:lo + width]

    @pl.when(it == 0)
    def _():
        zbuf_ref[0:SUBLANES, :] = jnp.zeros((SUBLANES, zbuf_ref.shape[1]), F32)

    @pl.when(it > 0)
    def _():
        zbuf_ref[0:SUBLANES, :] = zbuf_ref[tm:tm + SUBLANES, :]

    zbuf_ref[SUBLANES:SUBLANES + tm, 0:M_WIDTH] = proj(OFF_MQ, M_WIDTH)
    zbuf_ref[SUBLANES:SUBLANES + tm, M_WIDTH:2 * M_WIDTH] = proj(OFF_MK, M_WIDTH)

    def conv_silu(part):
        width = 2 * LANES
        cols = slice(part * width, (part + 1) * width)
        acc = jnp.broadcast_to(cb_ref[:, cols], (tm, width))
        for j in range(M_CONV):
            start = SUBLANES - (M_CONV - 1) + j
            acc = acc + zbuf_ref[start:start + tm, cols] * cw_ref[j:j + 1, cols]
        y = acc * jax.nn.sigmoid(acc)
        if part * width < M_WIDTH:
            qm_ref[0, :, cols] = y.astype(BF16)
        else:
            krows = slice(part * width - M_WIDTH, (part + 1) * width - M_WIDTH)
            km_ref[0, krows, :] = (y * M_HEAD_DIM ** -0.5).T.astype(BF16)

    conv_silu(0)
    vm_ref[0] = proj(OFF_MV, M_WIDTH).astype(BF16)
    conv_silu(1)
    om_ref[0] = proj(OFF_MO, M_WIDTH).astype(BF16)
    conv_silu(2)

    def rope(z):
        outs = []
        for p in range(A_WIDTH // LANES):
            zp = z[:, p * LANES:(p + 1) * LANES]
            lane = lax.broadcasted_iota(jnp.int32, zp.shape, 1)
            first_half = (lane % A_HEAD_DIM) < (A_HEAD_DIM // 2)
            rot = jnp.where(first_half,
                            pltpu.roll(zp, LANES - A_HEAD_DIM // 2, 1),
                            pltpu.roll(zp, A_HEAD_DIM // 2, 1))
            outs.append(zp * cos_ref[0] + rot * sin_ref[0])
        return outs

    q_parts = rope(proj(OFF_AQ, A_WIDTH))
    for p, qp in enumerate(q_parts):
        qa_ref[0, p * LANES:(p + 1) * LANES, :] = (qp * (LOG2_E * A_HEAD_DIM ** -0.5)).T.astype(BF16)
    conv_silu(3)

    k_parts = rope(proj(OFF_AK, A_WIDTH))
    kmean_ref[0] = jnp.zeros(kmean_ref.shape[1:], F32)
    for p, kp in enumerate(k_parts):
        ka_ref[0, :, p * LANES:(p + 1) * LANES] = kp.astype(BF16)
        for blk in range(tm // MOBA_BLOCK):
            kmean_ref[0, blk:blk + 1, p * LANES:(p + 1) * LANES] = jnp.mean(
                kp[blk * MOBA_BLOCK:(blk + 1) * MOBA_BLOCK], axis=0, keepdims=True)

    va = proj(OFF_AV, A_WIDTH)
    for p in range(A_WIDTH // LANES):
        va_ref[0, p * LANES:(p + 1) * LANES, :] = va[:, p * LANES:(p + 1) * LANES].T.astype(BF16)

    d = x.shape[1]
    for off, out_ref in ((OFF_GA, sga_ref), (OFF_GB, sgb_ref)):
        for c0 in range(0, d, 512):
            out_ref[0, :, c0:c0 + 512] = jax.nn.sigmoid(proj(off + c0, 512)).astype(BF16)

    zif_ref[0] = proj(OFF_IF, LANES)


def _in_proj(x, mod3, norm_g, weights, biases, conv_w, conv_b, cos_t, sin_t):
    bsz, seq, d = x.shape
    tm = IN_ROW_TILE
    nt = seq // tm
    row = lambda b, i: (b, i, 0)

    def rows(width, dtype):
        return (pl.BlockSpec((1, tm, width), row), jax.ShapeDtypeStruct((bsz, seq, width), dtype))

    cols = (pl.BlockSpec((1, A_WIDTH, tm), lambda b, i: (b, 0, i)),
            jax.ShapeDtypeStruct((bsz, A_WIDTH, seq), BF16))
    outs = ([rows(M_WIDTH, BF16), cols] + [rows(M_WIDTH, BF16)] * 2 + [cols, rows(A_WIDTH, BF16), cols]
            + [rows(d, BF16)] * 2 + [rows(LANES, F32)])
    outs.append((pl.BlockSpec((1, SUBLANES, A_WIDTH), lambda b, i: (b * nt + i, 0, 0)),
                 jax.ShapeDtypeStruct((bsz * nt, SUBLANES, A_WIDTH), F32)))
    return pl.pallas_call(
        _in_proj_kernel,
        grid=(bsz, nt),
        in_specs=[pl.BlockSpec((1, tm, d), row),
                  pl.BlockSpec((1, N_MOD, d), lambda b, i: (b, 0, 0)),
                  _const_spec((1, d)),
                  *[_const_spec(t.shape) for t in (*weights, *biases)],
                  _const_spec(conv_w.shape),
                  _const_spec(conv_b.shape),
                  pl.BlockSpec((1, tm, LANES), row),
                  pl.BlockSpec((1, tm, LANES), row)],
        out_specs=[o[0] for o in outs],
        out_shape=[o[1] for o in outs],
        scratch_shapes=[pltpu.VMEM((tm, d), BF16),
                        pltpu.VMEM((tm + SUBLANES, 2 * M_WIDTH), F32)],
        compiler_params=_cparams(("arbitrary", "arbitrary")),
        name="in_proj",
    )(x, mod3, norm_g, *weights, *biases, conv_w, conv_b, cos_t, sin_t)


def _split3(x):
    hi = x.astype(BF16)
    r1 = x - hi.astype(F32)
    mid = r1.astype(BF16)
    lo = (r1 - mid.astype(F32)).astype(BF16)
    return hi, mid, lo


def _mlstm_kernel(q_ref, kt_ref, v_ref, o_ref, zif_ref, g_ref, ym_ref, ct_ref, m_ref, s_ref, sw_ref, mt_ref):
    bsz, L = q_ref.shape[0], q_ref.shape[1]
    dh = M_HEAD_DIM

    @pl.when(pl.program_id(0) == 0)
    def _():
        ct_ref[...] = jnp.zeros(ct_ref.shape, F32)
        m_ref[...] = jnp.zeros(m_ref.shape, F32)

    r_i = lax.broadcasted_iota(jnp.int32, (L, L), 0)
    c_i = lax.broadcasted_iota(jnp.int32, (L, L), 1)
    causal = c_i <= r_i
    triu = jnp.where(r_i <= c_i, 1.0, 0.0).astype(BF16)
    ones = jnp.ones((L, dh), BF16)
    gate_row = lax.broadcasted_iota(jnp.int32, (SUBLANES, L), 0)

    gate_logs = []
    for b in range(bsz):
        gates = zif_ref[b].T[0:SUBLANES, :]
        log_f = jnp.minimum(gates, 0.0) - jnp.log1p(jnp.exp(-jnp.abs(gates)))
        log_f = jnp.where(gate_row >= M_HEADS, log_f, 0.0)
        bcum = sum(_dot(part, triu) for part in _split3(log_f))
        xrow = jnp.where(gate_row < M_HEADS, gates, bcum) * LOG2_E
        xcol = jnp.concatenate([xrow, jnp.zeros((LANES - SUBLANES, L), F32)], axis=0).T
        gate_logs.append((xrow, xcol))

    chains = [(b, h) for b in range(bsz) for h in range(M_HEADS)]

    def gate_terms(b, h):
        xrow, xcol = gate_logs[b]
        li_col = xcol[:, h:h + 1]
        b_col = xcol[:, M_HEADS + h:M_HEADS + h + 1]
        li_row = xrow[h:h + 1, :]
        b_row = xrow[M_HEADS + h:M_HEADS + h + 1, :]
        return li_col, b_col, li_row, b_row, b_col[L - 1:L, :]

    def v_aug(b, h):
        return jnp.concatenate([v_ref[b, :, h * dh:(h + 1) * dh], ones], axis=1)

    for st, (b, h) in enumerate(chains):
        hs = slice(h * dh, (h + 1) * dh)
        s_ref[st] = _dot(q_ref[b, :, hs], kt_ref[b, hs, :])

    for st, (b, h) in enumerate(chains):
        _, b_col, li_row, b_row, _ = gate_terms(b, h)
        dmat = jnp.where(causal, b_col + (li_row - b_row), -jnp.inf)
        inter = b_col + m_ref[st:st + 1, 0:1]
        m_t = jnp.maximum(inter, jnp.max(dmat, axis=1, keepdims=True))
        sw_ref[st] = (s_ref[st] * jnp.exp2(dmat - m_t)).astype(BF16)
        mt_ref[st, 0] = jnp.broadcast_to(m_t, (L, LANES))
        mt_ref[st, 1] = jnp.broadcast_to(inter - m_t, (L, LANES))

    for st, (b, h) in enumerate(chains):
        hs = slice(h * dh, (h + 1) * dh)
        m_t = mt_ref[st, 0]
        inter_w = jnp.exp2(mt_ref[st, 1])
        qc = _dot(q_ref[b, :, hs], ct_ref[st].astype(BF16))
        nd = jnp.concatenate([inter_w, inter_w], axis=1) * qc + _dot(sw_ref[st], v_aug(b, h))
        hval = nd[:, :dh] / jnp.maximum(jnp.abs(nd[:, dh:]), jnp.exp2(-m_t))
        hn = _rms(hval, g_ref[:, hs])
        ym_ref[b, :, hs] = (hn * jax.nn.sigmoid(o_ref[b, :, hs].astype(F32))).astype(BF16)

    for st, (b, h) in enumerate(chains):
        hs = slice(h * dh, (h + 1) * dh)
        _, _, li_row, b_row, b_last = gate_terms(b, h)
        m_prev = m_ref[st:st + 1, 0:1]
        a_row = b_last - b_row + li_row
        m_new = jnp.maximum(b_last + m_prev, jnp.max(a_row, axis=1, keepdims=True))
        kw_t = (kt_ref[b, hs, :].astype(F32) * jnp.exp2(a_row - m_new)).astype(BF16)
        ct_ref[st] = jnp.exp2(b_last + m_prev - m_new) * ct_ref[st] + _dot(kw_t, v_aug(b, h))
        m_ref[st:st + 1, :] = jnp.broadcast_to(m_new, (1, LANES))


def _mlstm(qm, kmt, vm, om, zif, m_norm_g):
    bsz, seq, width = qm.shape
    L = MLSTM_CHUNK
    row = lambda c: (0, c, 0)
    rows = pl.BlockSpec((bsz, L, width), row)
    return pl.pallas_call(
        _mlstm_kernel,
        grid=(seq // L,),
        in_specs=[rows, pl.BlockSpec((bsz, width, L), lambda c: (0, 0, c)), rows, rows,
                  pl.BlockSpec((bsz, L, LANES), row), _const_spec((1, width))],
        out_specs=pl.BlockSpec((bsz, L, width), row),
        out_shape=jax.ShapeDtypeStruct((bsz, seq, width), BF16),
        scratch_shapes=[pltpu.VMEM((bsz * M_HEADS, M_HEAD_DIM, 2 * M_HEAD_DIM), F32),
                        pltpu.VMEM((bsz * M_HEADS, LANES), F32),
                        pltpu.VMEM((bsz * M_HEADS, L, L), F32),
                        pltpu.VMEM((bsz * M_HEADS, L, L), BF16),
                        pltpu.VMEM((bsz * M_HEADS, 2, L, LANES), F32)],
        compiler_params=_cparams(("arbitrary",)),
        name="mlstm",
    )(qm, kmt, vm, om, zif, m_norm_g)


def _moba_kernel(qt_ref, k_ref, vt_ref, kmean_ref, o_ref, qaug_ref, s_ref, p_ref, acc_ref):
    tq = blk = MOBA_BLOCK
    n_tiles = qt_ref.shape[2] // tq
    nb = k_ref.shape[1] // blk
    dh = A_HEAD_DIM
    feat = lax.broadcasted_iota(jnp.int32, (LANES, tq), 0)
    kmean = kmean_ref[0]
    blk_id = lax.broadcasted_iota(jnp.int32, (nb, tq), 0)
    lane_k = lax.broadcasted_iota(jnp.int32, (blk, LANES), 1)
    key_i = lax.broadcasted_iota(jnp.int32, (blk, 2 * tq), 0)
    qry_i = lax.broadcasted_iota(jnp.int32, (blk, 2 * tq), 1) & (tq - 1)
    ones_rows = jnp.ones((ACC_ROWS - dh, blk), BF16)

    def fold8(x, op):
        out = x[0:SUBLANES]
        for r in range(SUBLANES, x.shape[0], SUBLANES):
            out = op(out, x[r:r + SUBLANES])
        return out

    def make_tile(t):
        i = pl.program_id(2) * n_tiles + t
        qaug_t, s_t, p_t, acc_t = qaug_ref.at[t], s_ref.at[t], p_ref.at[t], acc_ref.at[t]

        def put_scores(j, slot):
            st = pl.multiple_of(j * blk, blk)
            onehot = jnp.where(lane_k == j, 1.0, 0.0).astype(BF16)
            k_aug = jnp.concatenate([k_ref[0, pl.ds(st, blk), :], onehot], axis=1)
            s_t[slot] = _dot(k_aug, qaug_t[...])
            return jnp.max(fold8(s_t[slot], jnp.maximum), axis=0, keepdims=True)

        def softmax_into(s_slot, p_slot, m_new):
            rows = 4 * SUBLANES
            for r in range(0, blk, rows):
                x = (s_t[s_slot, r:r + rows, :] - m_new).astype(BF16)
                p_t[p_slot, r:r + rows, :] = jnp.exp2(x)

        def add_pv(slot, vidx, alpha):
            st = pl.multiple_of(vidx * blk, blk)
            for hh in range(2):
                qs = slice(hh * tq, (hh + 1) * tq)
                v_aug = jnp.concatenate([vt_ref[0, hh * dh:(hh + 1) * dh, pl.ds(st, blk)], ones_rows], axis=0)
                acc_t[hh] = alpha[:, qs] * acc_t[hh] + _dot(v_aug, p_t[slot, :, qs])

        def prologue():
            qt = qt_ref[0, :, t * tq:(t + 1) * tq].astype(F32)
            valid = blk_id < i
            q_own, q_aug = [], []
            for hh in range(2):
                in_head = (feat >= hh * dh) & (feat < (hh + 1) * dh)
                qh = jnp.where(in_head, qt, 0.0).astype(BF16)
                g = jnp.where(valid, _dot(kmean, qh), -jnp.inf)
                sel = jnp.zeros(g.shape, jnp.bool_)
                for _ in range(MOBA_TOPK):
                    mx = jnp.max(g, axis=0, keepdims=True)
                    idx = jnp.min(jnp.where(g == mx, blk_id, nb), axis=0, keepdims=True)
                    pick = (blk_id == idx) & valid
                    sel = sel | pick
                    g = jnp.where(pick, -jnp.inf, g)
                bias = jnp.where(sel, 0.0, NEG_BIG).astype(BF16)
                q_own.append(qh)
                q_aug.append(jnp.concatenate([qh, bias, jnp.zeros((LANES - nb, tq), BF16)], axis=0))
            qaug_t[...] = jnp.concatenate(q_aug, axis=1)

            start = pl.multiple_of(i * blk, blk)
            s = _dot(k_ref[0, pl.ds(start, blk), :], jnp.concatenate(q_own, axis=1))
            own = MOBA_SCORE_SLOTS - 1
            s_t[own] = jnp.where(key_i <= qry_i, s, NEG_BIG)
            m0 = jnp.max(fold8(s_t[own], jnp.maximum), axis=0, keepdims=True)
            softmax_into(own, 1, m0)
            acc_t[...] = jnp.zeros(acc_t.shape, F32)
            cmax0 = put_scores(0, 0)
            return cmax0, jnp.ones_like(m0), i, m0

        def step(j, u, state):
            cmax, alpha_pend, vidx, m = state
            cmax_next = put_scores(jnp.minimum(j + 1, nb - 1), (u + 1) % MOBA_SCORE_SLOTS)
            add_pv((u + 1) % 2, vidx, alpha_pend)
            m_new = jnp.maximum(m, cmax)
            softmax_into(u % MOBA_SCORE_SLOTS, u % 2, m_new)
            return cmax_next, jnp.exp2(m - m_new), j, m_new

        def run(first_block, u0, n_trips, unroll, state):
            def body(trip, st):
                for u in range(unroll):
                    st = step(first_block + unroll * trip + u, u0 + u, st)
                return st
            return lax.fori_loop(0, n_trips, body, state)

        def loops(state):
            done = 0
            for unroll in MOBA_UNROLLS:
                trips = (i - done) // unroll
                state = run(done, 0, trips, unroll, state)
                done = done + trips * unroll
            pairs = (i - done + 1) // 2
            state = run(done, 0, jnp.minimum(pairs, 1), 2, state)
            return run(done + 2, 2, pairs - jnp.minimum(pairs, 1), 2, state)

        def epilogue(state):
            _, alpha_pend, vidx, _ = state
            add_pv(1, vidx, alpha_pend)
            out = jnp.concatenate([acc_t[hh, :dh, :] / acc_t[hh, dh:dh + 1, :] for hh in range(2)],
                                  axis=0)
            o_ref[0, t * tq:(t + 1) * tq, :] = out.T.astype(BF16)

        return prologue, loops, epilogue

    tiles = [make_tile(t) for t in range(n_tiles)]
    states = [prologue() for prologue, _, _ in tiles]
    states = [loops(state) for (_, loops, _), state in zip(tiles, states)]
    for (_, _, epilogue), state in zip(tiles, states):
        epilogue(state)


def _moba(qat, ka, vat, kmean):
    bsz, width, seq = qat.shape
    tq = MOBA_BLOCK
    tw = MOBA_TILES * tq
    npair = width // LANES
    return pl.pallas_call(
        _moba_kernel,
        grid=(bsz, npair, seq // tw),
        in_specs=[pl.BlockSpec((1, LANES, tw), lambda b, p, i: (b, p, i)),
                  pl.BlockSpec((1, seq, LANES), lambda b, p, i: (b, 0, p)),
                  pl.BlockSpec((1, LANES, seq), lambda b, p, i: (b, p, 0)),
                  pl.BlockSpec((1, kmean.shape[1], LANES), lambda b, p, i: (b, 0, p))],
        out_specs=pl.BlockSpec((1, tw, LANES), lambda b, p, i: (b, i, p)),
        out_shape=jax.ShapeDtypeStruct((bsz, seq, width), BF16),
        scratch_shapes=[pltpu.VMEM((MOBA_TILES, 2 * LANES, 2 * tq), BF16),
                        pltpu.VMEM((MOBA_TILES, MOBA_SCORE_SLOTS, MOBA_BLOCK, 2 * tq), F32),
                        pltpu.VMEM((MOBA_TILES, 2, MOBA_BLOCK, 2 * tq), BF16),
                        pltpu.VMEM((MOBA_TILES, 2, ACC_ROWS, tq), F32)],
        compiler_params=_cparams(("arbitrary", "arbitrary", "arbitrary")),
        name="moba",
    )(qat, ka, vat, kmean)


def _tail_kernel(x_ref, ym_ref, ya_ref, sga_ref, sgb_ref, mod_ref, g2_ref, gf_ref,
                 pm_ref, pa_ref, wo_ref, wg_ref, wu_ref, wd_ref, o_ref, *, final_norm):
    gate1 = mod_ref[0, 2:3, :]
    sh = mod_ref[0, 3:4, :]
    sc = mod_ref[0, 4:5, :]
    gate = mod_ref[0, 5:6, :]

    merged = (sga_ref[0].astype(F32) * _dot(ym_ref[0], pm_ref[...])
              + sgb_ref[0].astype(F32) * _dot(ya_ref[0], pa_ref[...]))
    x = x_ref[0] + gate1 * _dot(merged.astype(BF16), wo_ref[...])

    h = (_rms(x, g2_ref[...]) * (1.0 + sc) + sh).astype(BF16)
    d_ff = wg_ref.shape[1]
    step = 256
    acc = jnp.zeros(x.shape, F32)
    for c0 in range(0, d_ff, step):
        a = _dot(h, wg_ref[:, c0:c0 + step])
        u = _dot(h, wu_ref[:, c0:c0 + step])
        f = (a * jax.nn.sigmoid(a) * u).astype(BF16)
        acc = acc + _dot(f, wd_ref[c0:c0 + step, :])
    y = x + gate * acc
    o_ref[0] = _rms(y, gf_ref[...]) if final_norm else y


def _tail(x, ym, ya, sga, sgb, mod3, norm2_g, normf_g, weights, final_norm):
    bsz, seq, d = x.shape
    tm = ROW_TILE
    row = lambda b, i: (b, i, 0)
    return pl.pallas_call(
        functools.partial(_tail_kernel, final_norm=final_norm),
        grid=(bsz, seq // tm),
        in_specs=[pl.BlockSpec((1, tm, d), row),
                  pl.BlockSpec((1, tm, ym.shape[2]), row),
                  pl.BlockSpec((1, tm, ya.shape[2]), row),
                  pl.BlockSpec((1, tm, d), row),
                  pl.BlockSpec((1, tm, d), row),
                  pl.BlockSpec((1, N_MOD, d), lambda b, i: (b, 0, 0)),
                  _const_spec((1, d)), _const_spec((1, d)),
                  *[_const_spec(w.shape) for w in weights]],
        out_specs=pl.BlockSpec((1, tm, d), row),
        out_shape=jax.ShapeDtypeStruct((bsz, seq, d), F32),
        compiler_params=_cparams(("arbitrary", "arbitrary")),
        name="merge_ffn_out",
    )(x, ym, ya, sga, sgb, mod3, norm2_g, normf_g, *weights)


def _in_proj_params(w_in, b_in, layer):
    m_end = 4 * M_WIDTH
    if_end = m_end + 2 * M_HEADS
    n_pad = LANES - 2 * M_HEADS
    wt = jnp.swapaxes(w_in[layer], 0, 1)
    b = b_in[layer].reshape(1, -1)
    weights = (wt[:m_end].astype(BF16), wt[if_end:].astype(BF16),
               jnp.pad(wt[m_end:if_end], ((0, n_pad), (0, 0))).astype(BF16))
    biases = (b[:, :m_end], b[:, if_end:], jnp.pad(b[:, m_end:if_end], ((0, 0), (0, n_pad))))
    return weights, biases


def kernel(x, c, positions, ada_w, ada_b, norm1_g, norm2_g, normf_g, w_in, b_in, conv_w, conv_b,
           m_norm_g, p_mlstm, p_moba, w_out, w_gate, w_up, w_down):
    bsz, seq, d = x.shape
    depth = ada_w.shape[0]
    n_blocks = seq // MOBA_BLOCK
    cos_t, sin_t = _rope_tables(positions)
    for l in range(depth):
        mod3 = _ada_mod(c, ada_w, ada_b, l).reshape(bsz, N_MOD, d)
        weights, biases = _in_proj_params(w_in, b_in, l)
        (qm, km, vm, om, qa, ka, va, sga, sgb, zif, kmean_tiles) = _in_proj(
            x, mod3, norm1_g[l].reshape(1, d), weights, biases, conv_w[l],
            conv_b[l].reshape(1, -1), cos_t, sin_t)
        ym = _mlstm(qm, km, vm, om, zif, m_norm_g[l].reshape(1, -1))
        kmean = kmean_tiles[:, :IN_ROW_TILE // MOBA_BLOCK].reshape(bsz, n_blocks, A_WIDTH)
        ya = _moba(qa, ka, va, kmean.astype(BF16))
        tail_weights = [w[l].astype(BF16) for w in (p_mlstm, p_moba, w_out, w_gate, w_up, w_down)]
        x = _tail(x, ym, ya, sga, sgb, mod3, norm2_g[l].reshape(1, d), normf_g.reshape(1, d),
                  tail_weights, final_norm=(l == depth - 1))
    return x
```

```python
import functools

import jax
import jax.numpy as jnp
from jax import lax
from jax.experimental import pallas as pl
from jax.experimental.pallas import tpu as pltpu

F32 = jnp.float32
BF16 = jnp.bfloat16

M_HEADS = 4
M_HEAD_DIM = 128
M_WIDTH = M_HEADS * M_HEAD_DIM
M_CONV = 4
A_HEADS = 8
A_HEAD_DIM = 64
A_WIDTH = A_HEADS * A_HEAD_DIM
MOBA_BLOCK = 256
MOBA_TOPK = 3
ROPE_THETA = 10000.0
N_MOD = 6
RMS_EPS = 1e-6

LANES = 128
SUBLANES = 8
VMEM_LIMIT = 56 * 1024 * 1024

ROW_TILE = 512
IN_ROW_TILE = 512
MLSTM_CHUNK = 512
NEG_BIG = -1e30
LOG2_E = 1.4426950408889634
MOBA_TILES = 8
MOBA_SCORE_SLOTS = 4
MOBA_UNROLLS = (8, 4)
ACC_ROWS = A_HEAD_DIM + 16

OFF_MQ, OFF_MK, OFF_MV, OFF_MO = 0, 512, 1024, 1536
OFF_AQ, OFF_AK, OFF_AV = 2048, 2560, 3072
OFF_GA, OFF_GB = 3584, 4608
OFF_IF = 5632
D_IN_PACKED = 5760


def _cparams(sem, flags=None):
    return pltpu.CompilerParams(dimension_semantics=sem, vmem_limit_bytes=VMEM_LIMIT, flags=flags)


def _const_spec(shape):
    nd = len(shape)
    return pl.BlockSpec(shape, lambda *_: (0,) * nd, pipeline_mode=pl.Buffered(1))


def _rms(x, g):
    ms = jnp.mean(x * x, axis=-1, keepdims=True)
    return x * lax.rsqrt(ms + RMS_EPS) * g


def _dot(a, b):
    return jnp.dot(a, b, preferred_element_type=F32)


def _dot_nt(a, b):
    return lax.dot_general(a, b, (((1,), (1,)), ((), ())), preferred_element_type=F32)


def _ada_kernel(c_ref, w_ref, b_ref, o_ref):
    c = c_ref[...]
    ca = c * jax.nn.sigmoid(c)
    o_ref[...] = jnp.dot(ca, w_ref[...], precision=lax.Precision.HIGHEST,
                         preferred_element_type=F32) + b_ref[...]


def _ada_mod(c, ada_w, ada_b, layer):
    bsz, d = c.shape
    depth, _, n = ada_w.shape
    tn = d
    return pl.pallas_call(
        _ada_kernel,
        grid=(n // tn,),
        in_specs=[pl.BlockSpec((bsz, d), lambda j: (0, 0)),
                  pl.BlockSpec((None, d, tn), lambda j: (layer, 0, j)),
                  pl.BlockSpec((None, 1, tn), lambda j: (layer, 0, j))],
        out_specs=pl.BlockSpec((bsz, tn), lambda j: (0, j)),
        out_shape=jax.ShapeDtypeStruct((bsz, n), F32),
        compiler_params=_cparams(("arbitrary",)),
        name="ada_mod",
    )(c, ada_w, ada_b.reshape(depth, 1, n))


def _rope_kernel(pos_ref, freq_ref, cos_ref, sin_ref):
    ang = freq_ref[...] * pos_ref[0].astype(F32)
    c = jnp.cos(ang)
    s = jnp.sin(ang)
    cos_ref[0] = jnp.concatenate([c, c, c, c], axis=0).T
    sin_ref[0] = jnp.concatenate([-s, s, -s, s], axis=0).T


def _rope_tables(positions):
    bsz, seq = positions.shape
    half = A_HEAD_DIM // 2
    inv_freq = (ROPE_THETA ** (-jnp.arange(half, dtype=F32) / half)).reshape(half, 1)
    ts = 2048
    out = jax.ShapeDtypeStruct((bsz, seq, LANES), F32)
    return pl.pallas_call(
        _rope_kernel,
        grid=(bsz, seq // ts),
        in_specs=[pl.BlockSpec((1, 1, ts), lambda b, i: (b, 0, i)),
                  pl.BlockSpec((half, 1), lambda b, i: (0, 0))],
        out_specs=[pl.BlockSpec((1, ts, LANES), lambda b, i: (b, i, 0))] * 2,
        out_shape=[out, out],
        compiler_params=_cparams(("arbitrary", "arbitrary")),
        name="rope_table",
    )(positions.reshape(bsz, 1, seq), inv_freq)


def _in_proj_kernel(x_ref, mod_ref, g_ref, wm_ref, wa_ref, wif_ref, bm_ref, ba_ref, bif_ref,
                    cw_ref, cb_ref, cos_ref, sin_ref,
                    qm_ref, km_ref, vm_ref, om_ref, qa_ref, ka_ref, va_ref,
                    sga_ref, sgb_ref, zif_ref, kmean_ref, hb_ref, zbuf_ref):
    tm = x_ref.shape[1]
    it = pl.program_id(1)

    x = x_ref[0]
    sh = mod_ref[0, 0:1, :]
    sc = mod_ref[0, 1:2, :]
    hb_ref[...] = (_rms(x, g_ref[...]) * (1.0 + sc) + sh).astype(BF16)

    def proj(off, width):
        for base, w_ref, b_ref in ((OFF_IF, wif_ref, bif_ref), (OFF_AQ, wa_ref, ba_ref), (OFF_MQ, wm_ref, bm_ref)):
            if off >= base:
                lo = off - base
                return _dot_nt(hb_ref[...], w_ref[lo:lo + width, :]) + b_ref[:, lo:lo + width]

    @pl.when(it == 0)
    def _():
        zbuf_ref[0:SUBLANES, :] = jnp.zeros((SUBLANES, zbuf_ref.shape[1]), F32)

    @pl.when(it > 0)
    def _():
        zbuf_ref[0:SUBLANES, :] = zbuf_ref[tm:tm + SUBLANES, :]

    zbuf_ref[SUBLANES:SUBLANES + tm, 0:M_WIDTH] = proj(OFF_MQ, M_WIDTH)
    zbuf_ref[SUBLANES:SUBLANES + tm, M_WIDTH:2 * M_WIDTH] = proj(OFF_MK, M_WIDTH)

    def conv_silu(part):
        width = 2 * LANES
        cols = slice(part * width, (part + 1) * width)
        acc = jnp.broadcast_to(cb_ref[:, cols], (tm, width))
        for j in range(M_CONV):
            start = SUBLANES - (M_CONV - 1) + j
            acc = acc + zbuf_ref[start:start + tm, cols] * cw_ref[j:j + 1, cols]
        y = acc * jax.nn.sigmoid(acc)
        if part * width < M_WIDTH:
            qm_ref[0, :, cols] = y.astype(BF16)
        else:
            krows = slice(part * width - M_WIDTH, (part + 1) * width - M_WIDTH)
            km_ref[0, krows, :] = (y * M_HEAD_DIM ** -0.5).T.astype(BF16)

    conv_silu(0)
    vm_ref[0] = proj(OFF_MV, M_WIDTH).astype(BF16)
    conv_silu(1)
    om_ref[0] = proj(OFF_MO, M_WIDTH).astype(BF16)
    conv_silu(2)

    def rope(z):
        outs = []
        for p in range(A_WIDTH // LANES):
            zp = z[:, p * LANES:(p + 1) * LANES]
            lane = lax.broadcasted_iota(jnp.int32, zp.shape, 1)
            first_half = (lane % A_HEAD_DIM) < (A_HEAD_DIM // 2)
            rot = jnp.where(first_half,
                            pltpu.roll(zp, LANES - A_HEAD_DIM // 2, 1),
                            pltpu.roll(zp, A_HEAD_DIM // 2, 1))
            outs.append(zp * cos_ref[0] + rot * sin_ref[0])
        return outs

    q_parts = rope(proj(OFF_AQ, A_WIDTH))
    for p, qp in enumerate(q_parts):
        qa_ref[0, p * LANES:(p + 1) * LANES, :] = (qp * (LOG2_E * A_HEAD_DIM ** -0.5)).T.astype(BF16)
    conv_silu(3)

    k_parts = rope(proj(OFF_AK, A_WIDTH))
    kmean_ref[0] = jnp.zeros(kmean_ref.shape[1:], F32)
    for p, kp in enumerate(k_parts):
        ka_ref[0, :, p * LANES:(p + 1) * LANES] = kp.astype(BF16)
        for blk in range(tm // MOBA_BLOCK):
            kmean_ref[0, blk:blk + 1, p * LANES:(p + 1) * LANES] = jnp.mean(
                kp[blk * MOBA_BLOCK:(blk + 1) * MOBA_BLOCK], axis=0, keepdims=True)

    va = proj(OFF_AV, A_WIDTH)
    for p in range(A_WIDTH // LANES):
        va_ref[0, p * LANES:(p + 1) * LANES, :] = va[:, p * LANES:(p + 1) * LANES].T.astype(BF16)

    d = x.shape[1]
    for off, out_ref in ((OFF_GA, sga_ref), (OFF_GB, sgb_ref)):
        for c0 in range(0, d, 512):
            out_ref[0, :, c0:c0 + 512] = jax.nn.sigmoid(proj(off + c0, 512)).astype(BF16)

    zif_ref[0] = proj(OFF_IF, LANES)


def _in_proj(x, mod3, norm_g, weights, biases, conv_w, conv_b, cos_t, sin_t):
    bsz, seq, d = x.shape
    tm = IN_ROW_TILE
    nt = seq // tm
    row = lambda b, i: (b, i, 0)

    def rows(width, dtype):
        return (pl.BlockSpec((1, tm, width), row), jax.ShapeDtypeStruct((bsz, seq, width), dtype))

    cols = (pl.BlockSpec((1, A_WIDTH, tm), lambda b, i: (b, 0, i)),
            jax.ShapeDtypeStruct((bsz, A_WIDTH, seq), BF16))
    outs = ([rows(M_WIDTH, BF16), cols] + [rows(M_WIDTH, BF16)] * 2 + [cols, rows(A_WIDTH, BF16), cols]
            + [rows(d, BF16)] * 2 + [rows(LANES, F32)])
    outs.append((pl.BlockSpec((1, SUBLANES, A_WIDTH), lambda b, i: (b * nt + i, 0, 0)),
                 jax.ShapeDtypeStruct((bsz * nt, SUBLANES, A_WIDTH), F32)))
    return pl.pallas_call(
        _in_proj_kernel,
        grid=(bsz, nt),
        in_specs=[pl.BlockSpec((1, tm, d), row),
                  pl.BlockSpec((1, N_MOD, d), lambda b, i: (b, 0, 0)),
                  _const_spec((1, d)),
                  *[_const_spec(t.shape) for t in (*weights, *biases)],
                  _const_spec(conv_w.shape),
                  _const_spec(conv_b.shape),
                  pl.BlockSpec((1, tm, LANES), row),
                  pl.BlockSpec((1, tm, LANES), row)],
        out_specs=[o[0] for o in outs],
        out_shape=[o[1] for o in outs],
        scratch_shapes=[pltpu.VMEM((tm, d), BF16),
                        pltpu.VMEM((tm + SUBLANES, 2 * M_WIDTH), F32)],
        compiler_params=_cparams(("arbitrary", "arbitrary")),
        name="in_proj",
    )(x, mod3, norm_g, *weights, *biases, conv_w, conv_b, cos_t, sin_t)


def _split3(x):
    hi = x.astype(BF16)
    r1 = x - hi.astype(F32)
    mid = r1.astype(BF16)
    lo = (r1 - mid.astype(F32)).astype(BF16)
    return hi, mid, lo


def _mlstm_kernel(q_ref, kt_ref, v_ref, o_ref, zif_ref, g_ref, ym_ref, ct_ref, m_ref, s_ref, sw_ref, mt_ref):
    bsz, L = q_ref.shape[0], q_ref.shape[1]
    dh = M_HEAD_DIM

    @pl.when(pl.program_id(0) == 0)
    def _():
        ct_ref[...] = jnp.zeros(ct_ref.shape, F32)
        m_ref[...] = jnp.zeros(m_ref.shape, F32)

    r_i = lax.broadcasted_iota(jnp.int32, (L, L), 0)
    c_i = lax.broadcasted_iota(jnp.int32, (L, L), 1)
    causal = c_i <= r_i
    triu = jnp.where(r_i <= c_i, 1.0, 0.0).astype(BF16)
    ones = jnp.ones((L, dh), BF16)
    gate_row = lax.broadcasted_iota(jnp.int32, (SUBLANES, L), 0)

    gate_logs = []
    for b in range(bsz):
        gates = zif_ref[b].T[0:SUBLANES, :]
        log_f = jnp.minimum(gates, 0.0) - jnp.log1p(jnp.exp(-jnp.abs(gates)))
        log_f = jnp.where(gate_row >= M_HEADS, log_f, 0.0)
        bcum = sum(_dot(part, triu) for part in _split3(log_f))
        xrow = jnp.where(gate_row < M_HEADS, gates, bcum) * LOG2_E
        xcol = jnp.concatenate([xrow, jnp.zeros((LANES - SUBLANES, L), F32)], axis=0).T
        gate_logs.append((xrow, xcol))

    chains = [(b, h) for b in range(bsz) for h in range(M_HEADS)]

    def gate_terms(b, h):
        xrow, xcol = gate_logs[b]
        li_col = xcol[:, h:h + 1]
        b_col = xcol[:, M_HEADS + h:M_HEADS + h + 1]
        li_row = xrow[h:h + 1, :]
        b_row = xrow[M_HEADS + h:M_HEADS + h + 1, :]
        return li_col, b_col, li_row, b_row, b_col[L - 1:L, :]

    def v_aug(b, h):
        return jnp.concatenate([v_ref[b, :, h * dh:(h + 1) * dh], ones], axis=1)

    for st, (b, h) in enumerate(chains):
        hs = slice(h * dh, (h + 1) * dh)
        s_ref[st] = _dot(q_ref[b, :, hs], kt_ref[b, hs, :])

    for st, (b, h) in enumerate(chains):
        _, b_col, li_row, b_row, _ = gate_terms(b, h)
        dmat = jnp.where(causal, b_col + (li_row - b_row), -jnp.inf)
        inter = b_col + m_ref[st:st + 1, 0:1]
        m_t = jnp.maximum(inter, jnp.max(dmat, axis=1, keepdims=True))
        sw_ref[st] = (s_ref[st] * jnp.exp2(dmat - m_t)).astype(BF16)
        mt_ref[st, 0] = jnp.broadcast_to(m_t, (L, LANES))
        mt_ref[st, 1] = jnp.broadcast_to(inter - m_t, (L, LANES))

    for st, (b, h) in enumerate(chains):
        hs = slice(h * dh, (h + 1) * dh)
        m_t = mt_ref[st, 0]
        inter_w = jnp.exp2(mt_ref[st, 1])
        qc = _dot(q_ref[b, :, hs], ct_ref[st].astype(BF16))
        nd = jnp.concatenate([inter_w, inter_w], axis=1) * qc + _dot(sw_ref[st], v_aug(b, h))
        hval = nd[:, :dh] / jnp.maximum(jnp.abs(nd[:, dh:]), jnp.exp2(-m_t))
        hn = _rms(hval, g_ref[:, hs])
        ym_ref[b, :, hs] = (hn * jax.nn.sigmoid(o_ref[b, :, hs].astype(F32))).astype(BF16)

    for st, (b, h) in enumerate(chains):
        hs = slice(h * dh, (h + 1) * dh)
        _, _, li_row, b_row, b_last = gate_terms(b, h)
        m_prev = m_ref[st:st + 1, 0:1]
        a_row = b_last - b_row + li_row
        m_new = jnp.maximum(b_last + m_prev, jnp.max(a_row, axis=1, keepdims=True))
        kw_t = (kt_ref[b, hs, :].astype(F32) * jnp.exp2(a_row - m_new)).astype(BF16)
        ct_ref[st] = jnp.exp2(b_last + m_prev - m_new) * ct_ref[st] + _dot(kw_t, v_aug(b, h))
        m_ref[st:st + 1, :] = jnp.broadcast_to(m_new, (1, LANES))


def _mlstm(qm, kmt, vm, om, zif, m_norm_g):
    bsz, seq, width = qm.shape
    L = MLSTM_CHUNK
    row = lambda c: (0, c, 0)
    rows = pl.BlockSpec((bsz, L, width), row)
    return pl.pallas_call(
        _mlstm_kernel,
        grid=(seq // L,),
        in_specs=[rows, pl.BlockSpec((bsz, width, L), lambda c: (0, 0, c)), rows, rows,
                  pl.BlockSpec((bsz, L, LANES), row), _const_spec((1, width))],
        out_specs=pl.BlockSpec((bsz, L, width), row),
        out_shape=jax.ShapeDtypeStruct((bsz, seq, width), BF16),
        scratch_shapes=[pltpu.VMEM((bsz * M_HEADS, M_HEAD_DIM, 2 * M_HEAD_DIM), F32),
                        pltpu.VMEM((bsz * M_HEADS, LANES), F32),
                        pltpu.VMEM((bsz * M_HEADS, L, L), F32),
                        pltpu.VMEM((bsz * M_HEADS, L, L), BF16),
                        pltpu.VMEM((bsz * M_HEADS, 2, L, LANES), F32)],
        compiler_params=_cparams(("arbitrary",)),
        name="mlstm",
    )(qm, kmt, vm, om, zif, m_norm_g)


def _moba_kernel(qt_ref, k_ref, vt_ref, kmean_ref, o_ref, qaug_ref, s_ref, p_ref, acc_ref):
    tq = blk = MOBA_BLOCK
    n_tiles = qt_ref.shape[2] // tq
    nb = k_ref.shape[1] // blk
    dh = A_HEAD_DIM
    feat = lax.broadcasted_iota(jnp.int32, (LANES, tq), 0)
    kmean = kmean_ref[0]
    blk_id = lax.broadcasted_iota(jnp.int32, (nb, tq), 0)
    lane_k = lax.broadcasted_iota(jnp.int32, (blk, LANES), 1)
    key_i = lax.broadcasted_iota(jnp.int32, (blk, 2 * tq), 0)
    qry_i = lax.broadcasted_iota(jnp.int32, (blk, 2 * tq), 1) & (tq - 1)
    ones_rows = jnp.ones((ACC_ROWS - dh, blk), BF16)

    def fold8(x, op):
        out = x[0:SUBLANES]
        for r in range(SUBLANES, x.shape[0], SUBLANES):
            out = op(out, x[r:r + SUBLANES])
        return out

    def make_tile(t):
        i = pl.program_id(2) * n_tiles + t
        qaug_t, s_t, p_t, acc_t = qaug_ref.at[t], s_ref.at[t], p_ref.at[t], acc_ref.at[t]

        def put_scores(j, slot):
            st = pl.multiple_of(j * blk, blk)
            onehot = jnp.where(lane_k == j, 1.0, 0.0).astype(BF16)
            k_aug = jnp.concatenate([k_ref[0, pl.ds(st, blk), :], onehot], axis=1)
            s_t[slot] = _dot(k_aug, qaug_t[...])
            return jnp.max(fold8(s_t[slot], jnp.maximum), axis=0, keepdims=True)

        def softmax_into(s_slot, p_slot, m_new):
            rows = 4 * SUBLANES
            for r in range(0, blk, rows):
                x = (s_t[s_slot, r:r + rows, :] - m_new).astype(BF16)
                p_t[p_slot, r:r + rows, :] = jnp.exp2(x)

        def add_pv(slot, vidx, alpha):
            st = pl.multiple_of(vidx * blk, blk)
            for hh in range(2):
                qs = slice(hh * tq, (hh + 1) * tq)
                v_aug = jnp.concatenate([vt_ref[0, hh * dh:(hh + 1) * dh, pl.ds(st, blk)], ones_rows], axis=0)
                acc_t[hh] = alpha[:, qs] * acc_t[hh] + _dot(v_aug, p_t[slot, :, qs])

        def prologue():
            qt = qt_ref[0, :, t * tq:(t + 1) * tq].astype(F32)
            valid = blk_id < i
            q_own, q_aug, bias0 = [], [], []
            for hh in range(2):
                in_head = (feat >= hh * dh) & (feat < (hh + 1) * dh)
                qh = jnp.where(in_head, qt, 0.0).astype(BF16)
                g = jnp.where(valid, _dot(kmean, qh), -jnp.inf)
                sel = jnp.zeros(g.shape, jnp.bool_)
                for _ in range(MOBA_TOPK):
                    mx = jnp.max(g, axis=0, keepdims=True)
                    idx = jnp.min(jnp.where(g == mx, blk_id, nb), axis=0, keepdims=True)
                    pick = (blk_id == idx) & valid
                    sel = sel | pick
                    g = jnp.where(pick, -jnp.inf, g)
                bias = jnp.where(sel, 0.0, NEG_BIG).astype(BF16)
                q_own.append(qh)
                bias0.append(bias[0:1, :].astype(F32))
                q_aug.append(jnp.concatenate([qh, bias, jnp.zeros((LANES - nb, tq), BF16)], axis=0))
            qaug_t[...] = jnp.concatenate(q_aug, axis=1)

            start = pl.multiple_of(i * blk, blk)
            q_both = jnp.concatenate(q_own, axis=1)
            s = _dot(k_ref[0, pl.ds(start, blk), :], q_both)
            own = MOBA_SCORE_SLOTS - 1
            s_t[own] = jnp.where(key_i <= qry_i, s, NEG_BIG)
            m0 = jnp.max(fold8(s_t[own], jnp.maximum), axis=0, keepdims=True)
            softmax_into(own, 1, m0)
            acc_t[...] = jnp.zeros(acc_t.shape, F32)
            s_t[0] = _dot(k_ref[0, 0:blk, :], q_both) + jnp.concatenate(bias0, axis=1)
            cmax0 = jnp.max(fold8(s_t[0], jnp.maximum), axis=0, keepdims=True)
            return cmax0, jnp.ones_like(m0), i, m0

        def step(j, u, state):
            cmax, alpha_pend, vidx, m = state
            cmax_next = put_scores(jnp.minimum(j + 1, nb - 1), (u + 1) % MOBA_SCORE_SLOTS)
            add_pv((u + 1) % 2, vidx, alpha_pend)
            m_new = jnp.maximum(m, cmax)
            softmax_into(u % MOBA_SCORE_SLOTS, u % 2, m_new)
            return cmax_next, jnp.exp2(m - m_new), j, m_new

        def run(first_block, u0, n_trips, unroll, state):
            def body(trip, st):
                for u in range(unroll):
                    st = step(first_block + unroll * trip + u, u0 + u, st)
                return st
            return lax.fori_loop(0, n_trips, body, state)

        def loops(state):
            done = 0
            for unroll in MOBA_UNROLLS:
                trips = (i - done) // unroll
                state = run(done, 0, trips, unroll, state)
                done = done + trips * unroll
            pairs = (i - done + 1) // 2
            state = run(done, 0, jnp.minimum(pairs, 1), 2, state)
            return run(done + 2, 2, pairs - jnp.minimum(pairs, 1), 2, state)

        def epilogue(state):
            _, alpha_pend, vidx, _ = state
            add_pv(1, vidx, alpha_pend)
            out = jnp.concatenate([acc_t[hh, :dh, :] / acc_t[hh, dh:dh + 1, :] for hh in range(2)],
                                  axis=0)
            o_ref[0, t * tq:(t + 1) * tq, :] = out.T.astype(BF16)

        return prologue, loops, epilogue

    tiles = [make_tile(t) for t in range(n_tiles)]
    states = [prologue() for prologue, _, _ in tiles]
    states = [loops(state) for (_, loops, _), state in zip(tiles, states)]
    for (_, _, epilogue), state in zip(tiles, states):
        epilogue(state)


def _moba(qat, ka, vat, kmean):
    bsz, width, seq = qat.shape
    tq = MOBA_BLOCK
    tw = MOBA_TILES * tq
    npair = width // LANES
    return pl.pallas_call(
        _moba_kernel,
        grid=(bsz, npair, seq // tw),
        in_specs=[pl.BlockSpec((1, LANES, tw), lambda b, p, i: (b, p, i)),
                  pl.BlockSpec((1, seq, LANES), lambda b, p, i: (b, 0, p)),
                  pl.BlockSpec((1, LANES, seq), lambda b, p, i: (b, p, 0)),
                  pl.BlockSpec((1, kmean.shape[1], LANES), lambda b, p, i: (b, 0, p))],
        out_specs=pl.BlockSpec((1, tw, LANES), lambda b, p, i: (b, i, p)),
        out_shape=jax.ShapeDtypeStruct((bsz, seq, width), BF16),
        scratch_shapes=[pltpu.VMEM((MOBA_TILES, 2 * LANES, 2 * tq), BF16),
                        pltpu.VMEM((MOBA_TILES, MOBA_SCORE_SLOTS, MOBA_BLOCK, 2 * tq), F32),
                        pltpu.VMEM((MOBA_TILES, 2, MOBA_BLOCK, 2 * tq), BF16),
                        pltpu.VMEM((MOBA_TILES, 2, ACC_ROWS, tq), F32)],
        compiler_params=_cparams(("arbitrary", "arbitrary", "arbitrary")),
        name="moba",
    )(qat, ka, vat, kmean)


def _tail_kernel(x_ref, ym_ref, ya_ref, sga_ref, sgb_ref, mod_ref, g2_ref, gf_ref,
                 pm_ref, pa_ref, wo_ref, wg_ref, wu_ref, wd_ref, o_ref, *, final_norm):
    gate1 = mod_ref[0, 2:3, :]
    sh = mod_ref[0, 3:4, :]
    sc = mod_ref[0, 4:5, :]
    gate = mod_ref[0, 5:6, :]

    merged = (sga_ref[0].astype(F32) * _dot(ym_ref[0], pm_ref[...])
              + sgb_ref[0].astype(F32) * _dot(ya_ref[0], pa_ref[...]))
    x = x_ref[0] + gate1 * _dot(merged.astype(BF16), wo_ref[...])

    h = (_rms(x, g2_ref[...]) * (1.0 + sc) + sh).astype(BF16)
    d_ff = wg_ref.shape[1]
    step = 256
    acc = jnp.zeros(x.shape, F32)
    for c0 in range(0, d_ff, step):
        a = _dot(h, wg_ref[:, c0:c0 + step])
        u = _dot(h, wu_ref[:, c0:c0 + step])
        f = (a * jax.nn.sigmoid(a) * u).astype(BF16)
        acc = acc + _dot(f, wd_ref[c0:c0 + step, :])
    y = x + gate * acc
    o_ref[0] = _rms(y, gf_ref[...]) if final_norm else y


def _tail(x, ym, ya, sga, sgb, mod3, norm2_g, normf_g, weights, final_norm):
    bsz, seq, d = x.shape
    tm = ROW_TILE
    row = lambda b, i: (b, i, 0)
    return pl.pallas_call(
        functools.partial(_tail_kernel, final_norm=final_norm),
        grid=(bsz, seq // tm),
        in_specs=[pl.BlockSpec((1, tm, d), row),
                  pl.BlockSpec((1, tm, ym.shape[2]), row),
                  pl.BlockSpec((1, tm, ya.shape[2]), row),
                  pl.BlockSpec((1, tm, d), row),
                  pl.BlockSpec((1, tm, d), row),
                  pl.BlockSpec((1, N_MOD, d), lambda b, i: (b, 0, 0)),
                  _const_spec((1, d)), _const_spec((1, d)),
                  *[_const_spec(w.shape) for w in weights]],
        out_specs=pl.BlockSpec((1, tm, d), row),
        out_shape=jax.ShapeDtypeStruct((bsz, seq, d), F32),
        compiler_params=_cparams(("arbitrary", "arbitrary")),
        name="merge_ffn_out",
    )(x, ym, ya, sga, sgb, mod3, norm2_g, normf_g, *weights)


def _in_proj_params(w_in, b_in, layer):
    m_end = 4 * M_WIDTH
    if_end = m_end + 2 * M_HEADS
    n_pad = LANES - 2 * M_HEADS
    wt = jnp.swapaxes(w_in[layer], 0, 1)
    b = b_in[layer].reshape(1, -1)
    weights = (wt[:m_end].astype(BF16), wt[if_end:].astype(BF16),
               jnp.pad(wt[m_end:if_end], ((0, n_pad), (0, 0))).astype(BF16))
    biases = (b[:, :m_end], b[:, if_end:], jnp.pad(b[:, m_end:if_end], ((0, 0), (0, n_pad))))
    return weights, biases


def kernel(x, c, positions, ada_w, ada_b, norm1_g, norm2_g, normf_g, w_in, b_in, conv_w, conv_b,
           m_norm_g, p_mlstm, p_moba, w_out, w_gate, w_up, w_down):
    bsz, seq, d = x.shape
    depth = ada_w.shape[0]
    n_blocks = seq // MOBA_BLOCK
    cos_t, sin_t = _rope_tables(positions)
    for l in range(depth):
        mod3 = _ada_mod(c, ada_w, ada_b, l).reshape(bsz, N_MOD, d)
        weights, biases = _in_proj_params(w_in, b_in, l)
        (qm, km, vm, om, qa, ka, va, sga, sgb, zif, kmean_tiles) = _in_proj(
            x, mod3, norm1_g[l].reshape(1, d), weights, biases, conv_w[l],
            conv_b[l].reshape(1, -1), cos_t, sin_t)
        ym = _mlstm(qm, km, vm, om, zif, m_norm_g[l].reshape(1, -1))
        kmean = kmean_tiles[:, :IN_ROW_TILE // MOBA_BLOCK].reshape(bsz, n_blocks, A_WIDTH)
        ya = _moba(qa, ka, va, kmean.astype(BF16))
        tail_weights = [w[l].astype(BF16) for w in (p_mlstm, p_moba, w_out, w_gate, w_up, w_down)]
        x = _tail(x, ym, ya, sga, sgb, mod3, norm2_g[l].reshape(1, d), normf_g.reshape(1, d),
                  tail_weights, final_norm=(l == depth - 1))
    return x
```
